```python
import jax, jax.numpy as jnp
from jax import lax
import numpy as np

D_MODEL = 4096
BATCH = 1
SEQ = 8192
DEPTH = 2

MIX_WIDTH = D_MODEL
RET_WIDTH = MIX_WIDTH // 2
GLA_WIDTH = MIX_WIDTH - RET_WIDTH
RET_HEADS = 8
RET_DK = RET_WIDTH // RET_HEADS
RET_DV = RET_WIDTH // RET_HEADS
GLA_HEADS = 4
GLA_DV = GLA_WIDTH // GLA_HEADS
GLA_DK = GLA_DV // 2
GLA_KEY_WIDTH = GLA_HEADS * GLA_DK
GLA_GATE_RANK = 16
GLA_GATE_TAU = 16.0
_FFN_RAW = -(-8 * D_MODEL // 3)
FFN_HIDDEN = -(-_FFN_RAW // 256) * 256
RET_CHUNK = 128
GLA_CHUNK = 64
ROPE_BASE = 10000.0
EPS = 1e-6

IN_SPLITS = (RET_WIDTH, RET_WIDTH, RET_WIDTH, RET_WIDTH,
             GLA_KEY_WIDTH, GLA_KEY_WIDTH, GLA_WIDTH, GLA_WIDTH,
             GLA_GATE_RANK)
IN_WIDTH = sum(IN_SPLITS)

kernel_name = "hybrid_retention_gla_parallel_heads"


def rms_norm(x, gain):
    xf = x.astype(jnp.float32)
    y = xf * lax.rsqrt(jnp.mean(xf * xf, axis=-1, keepdims=True) + EPS)
    return (y * gain.astype(jnp.float32)).astype(x.dtype)


def apply_rotary(t, positions):
    half = t.shape[-1] // 2
    inv_freq = ROPE_BASE ** (-jnp.arange(half, dtype=jnp.float32) / half)
    ang = positions.astype(jnp.float32)[..., None] * inv_freq
    cos = jnp.cos(ang)[:, :, None, :]
    sin = jnp.sin(ang)[:, :, None, :]
    t1, t2 = t[..., :half], t[..., half:]
    return jnp.concatenate([t1 * cos - t2 * sin, t1 * sin + t2 * cos], axis=-1)


def to_chunks(t, c):
    b, s, h, d = t.shape
    return t.reshape(b, s // c, c, h, d).transpose(1, 0, 3, 2, 4)


def from_chunks(t):
    n, b, h, c, d = t.shape
    return t.transpose(1, 0, 3, 2, 4).reshape(b, n * c, h, d)


def retention_chunkwise(q, k, v):
    b, s, h, dk = q.shape
    dv = v.shape[-1]
    c = RET_CHUNK
    log_gamma = jnp.log1p(-jnp.exp2(-5.0 - jnp.arange(h, dtype=jnp.float32)))
    k = k * (dk ** -0.5)
    idx = jnp.arange(c, dtype=jnp.float32)
    diff = idx[:, None] - idx[None, :]
    decay = jnp.where(diff >= 0,
                      jnp.exp(log_gamma[:, None, None] * jnp.maximum(diff, 0.0)),
                      0.0)
    q_decay = jnp.exp(log_gamma[:, None] * (idx + 1.0))[None, :, :, None]
    k_decay = jnp.exp(log_gamma[:, None] * (c - 1.0 - idx))[None, :, :, None]
    chunk_decay = jnp.exp(log_gamma * c)[None, :, None, None]

    def step(state, inp):
        qb, kb, vb = inp
        scores = jnp.einsum('bhid,bhjd->bhij', qb, kb) * decay
        out = (jnp.einsum('bhij,bhjv->bhiv', scores, vb)
               + jnp.einsum('bhid,bhdv->bhiv', qb, state) * q_decay)
        state = state * chunk_decay + jnp.einsum('bhjd,bhjv->bhdv', kb * k_decay, vb)
        return state, out

    s0 = jnp.zeros((b, h, dk, dv), jnp.float32)
    _, o = lax.scan(step, s0, (to_chunks(q, c), to_chunks(k, c), to_chunks(v, c)))
    return from_chunks(o)


def gla_chunked(q, k, v, log_a):
    b, s, h, dk = q.shape
    dv = v.shape[-1]
    c = GLA_CHUNK
    q = q * (dk ** -0.5)
    causal = jnp.tril(jnp.ones((c, c), dtype=bool))[:, :, None]

    def step(state, inp):
        qb, kb, vb, ab = inp
        cum = jnp.cumsum(ab, axis=2)
        rel = cum[:, :, :, None, :] - cum[:, :, None, :, :]
        w = jnp.exp(jnp.where(causal, rel, -jnp.inf))
        attn = jnp.einsum('bhid,bhmd,bhimd->bhim', qb, kb, w)
        out = (jnp.einsum('bhim,bhmv->bhiv', attn, vb)
               + jnp.einsum('bhid,bhdv->bhiv', qb * jnp.exp(cum), state))
        last = cum[:, :, -1:, :]
        state = (state * jnp.exp(last)[:, :, 0, :, None]
                 + jnp.einsum('bhmd,bhmv->bhdv', kb * jnp.exp(last - cum), vb))
        return state, out

    s0 = jnp.zeros((b, h, dk, dv), jnp.float32)
    _, o = lax.scan(step, s0, (to_chunks(q, c), to_chunks(k, c),
                               to_chunks(v, c), to_chunks(log_a, c)))
    return from_chunks(o)


def head_group_norm(o, gain):
    mu = jnp.mean(o, axis=-1, keepdims=True)
    var = jnp.mean(jnp.square(o - mu), axis=-1, keepdims=True)
    y = (o - mu) * lax.rsqrt(var + EPS)
    return y.reshape(o.shape[0], o.shape[1], -1) * gain


def head_rms_norm(o, gain):
    y = o * lax.rsqrt(jnp.mean(o * o, axis=-1, keepdims=True) + EPS)
    return y.reshape(o.shape[0], o.shape[1], -1) * gain


def hybrid_mixer(h, positions, w_in, gla_w_up, gla_b, ret_gain, gla_gain, w_out):
    b, s, _ = h.shape
    proj = (h @ w_in).astype(jnp.float32)
    cuts = list(np.cumsum(IN_SPLITS)[:-1])
    rq, rk, rv, rg, gq, gk, gv, gg, ga = jnp.split(proj, cuts, axis=-1)

    rq = apply_rotary(rq.reshape(b, s, RET_HEADS, RET_DK), positions)
    rk = apply_rotary(rk.reshape(b, s, RET_HEADS, RET_DK), positions)
    rv = rv.reshape(b, s, RET_HEADS, RET_DV)
    r_out = retention_chunkwise(rq, rk, rv)
    r_out = jax.nn.silu(rg) * head_group_norm(r_out, ret_gain.astype(jnp.float32))

    gate_logits = ga @ gla_w_up.astype(jnp.float32) + gla_b.astype(jnp.float32)
    log_a = (jax.nn.log_sigmoid(gate_logits) / GLA_GATE_TAU).reshape(b, s, GLA_HEADS, GLA_DK)
    g_out = gla_chunked(gq.reshape(b, s, GLA_HEADS, GLA_DK),
                        gk.reshape(b, s, GLA_HEADS, GLA_DK),
                        gv.reshape(b, s, GLA_HEADS, GLA_DV), log_a)
    g_out = jax.nn.silu(gg) * head_rms_norm(g_out, gla_gain.astype(jnp.float32))

    mixed = jnp.concatenate([r_out, g_out], axis=-1).astype(h.dtype)
    return mixed @ w_out


def swiglu(h, w_gate, w_up, w_down):
    return (jax.nn.silu(h @ w_gate) * (h @ w_up)) @ w_down


def setup_inputs(seed: int = 0) -> dict:
    key = jax.random.key(seed)
    ks = jax.random.split(key, 14)
    f32 = jnp.float32
    x = jax.random.normal(ks[0], (BATCH, SEQ, D_MODEL), f32)
    positions = jnp.broadcast_to(jnp.arange(SEQ, dtype=jnp.int32), (BATCH, SEQ))
    mix_norm = 1.0 + 0.02 * jax.random.normal(ks[1], (DEPTH, D_MODEL), f32)
    w_in = jax.random.normal(ks[2], (DEPTH, D_MODEL, IN_WIDTH), f32) * D_MODEL ** -0.5
    gla_w_up = jax.random.normal(ks[3], (DEPTH, GLA_GATE_RANK, GLA_KEY_WIDTH), f32) * GLA_GATE_RANK ** -0.5
    gla_b = 0.1 * jax.random.normal(ks[4], (DEPTH, GLA_KEY_WIDTH), f32)
    ret_gain = 1.0 + 0.02 * jax.random.normal(ks[5], (DEPTH, RET_WIDTH), f32)
    gla_gain = 1.0 + 0.02 * jax.random.normal(ks[6], (DEPTH, GLA_WIDTH), f32)
    w_out = jax.random.normal(ks[7], (DEPTH, MIX_WIDTH, D_MODEL), f32) * MIX_WIDTH ** -0.5
    ffn_norm = 1.0 + 0.02 * jax.random.normal(ks[8], (DEPTH, D_MODEL), f32)
    w_gate = jax.random.normal(ks[9], (DEPTH, D_MODEL, FFN_HIDDEN), f32) * D_MODEL ** -0.5
    w_up = jax.random.normal(ks[10], (DEPTH, D_MODEL, FFN_HIDDEN), f32) * D_MODEL ** -0.5
    w_down = jax.random.normal(ks[11], (DEPTH, FFN_HIDDEN, D_MODEL), f32) * FFN_HIDDEN ** -0.5
    final_norm = 1.0 + 0.02 * jax.random.normal(ks[12], (D_MODEL,), f32)
    return {"x": x, "positions": positions, "mix_norm": mix_norm, "w_in": w_in,
            "gla_w_up": gla_w_up, "gla_b": gla_b, "ret_gain": ret_gain,
            "gla_gain": gla_gain, "w_out": w_out, "ffn_norm": ffn_norm,
            "w_gate": w_gate, "w_up": w_up, "w_down": w_down,
            "final_norm": final_norm}


def reference(x, positions, mix_norm, w_in, gla_w_up, gla_b, ret_gain, gla_gain,
              w_out, ffn_norm, w_gate, w_up, w_down, final_norm):
    for l in range(DEPTH):
        h = rms_norm(x, mix_norm[l])
        x = x + hybrid_mixer(h, positions, w_in[l], gla_w_up[l], gla_b[l],
                             ret_gain[l], gla_gain[l], w_out[l])
        h = rms_norm(x, ffn_norm[l])
        x = x + swiglu(h, w_gate[l], w_up[l], w_down[l])
    return rms_norm(x, final_norm)
```

```python
import functools

import numpy as np
import jax
import jax.numpy as jnp
from jax import lax
from jax.experimental import pallas as pl
from jax.experimental.pallas import tpu as pltpu

F32 = jnp.float32
BF16 = jnp.bfloat16

D_MODEL = 4096
DEPTH = 2
RET_HEADS = 8
RET_D = 256
RET_WIDTH = RET_HEADS * RET_D
GLA_HEADS = 4
GLA_DK = 256
GLA_DV = 512
GLA_KEY_WIDTH = GLA_HEADS * GLA_DK
GLA_WIDTH = GLA_HEADS * GLA_DV
GLA_GATE_RANK = 16
GLA_GATE_TAU = 16.0
FFN_HIDDEN = 11008
ROPE_BASE = 10000.0
EPS = 1e-6

MAIN_WIDTH = 4 * RET_WIDTH + 2 * GLA_KEY_WIDTH + 2 * GLA_WIDTH
LANES = 128
SUBLANES = 8
GATE_PAD = LANES
FFN_PAD = 11264

RET_CHUNK = 256
GLA_CHUNK = 256
GLA_LEVELS = (7, 6, 5, 4, 3)

VMEM_LIMIT = 56 * 1024 * 1024

NT_DIMS = (((1,), (1,)), ((), ()))
TN_DIMS = (((0,), (0,)), ((), ()))


def _params(*sem):
    return pltpu.CompilerParams(dimension_semantics=sem, vmem_limit_bytes=VMEM_LIMIT)


def _silu(x):
    return x * (1.0 / (1.0 + jnp.exp(-x)))


def _rmsnorm_kernel(x_ref, g_ref, o_ref):
    x = x_ref[...]
    ms = jnp.mean(x * x, axis=-1, keepdims=True)
    o_ref[...] = (x * lax.rsqrt(ms + EPS) * g_ref[...]).astype(o_ref.dtype)


def rmsnorm(x, gain, out_dtype, tm=256):
    s, d = x.shape
    return pl.pallas_call(
        _rmsnorm_kernel,
        grid=(s // tm,),
        in_specs=[pl.BlockSpec((tm, d), lambda i: (i, 0)),
                  pl.BlockSpec((1, d), lambda i: (0, 0))],
        out_specs=pl.BlockSpec((tm, d), lambda i: (i, 0)),
        out_shape=jax.ShapeDtypeStruct((s, d), out_dtype),
        compiler_params=_params("arbitrary"),
        name="rmsnorm",
    )(x, gain.reshape(1, d))


def _mm_kernel(*refs, n_pairs, has_resid):
    o_ref = refs[-1]
    acc = jnp.dot(refs[0][...], refs[1][...], preferred_element_type=F32)
    for p in range(1, n_pairs):
        acc = acc + jnp.dot(refs[2 * p][...], refs[2 * p + 1][...], preferred_element_type=F32)
    if has_resid:
        acc = acc + refs[2 * n_pairs][...]
    o_ref[...] = acc.astype(o_ref.dtype)


def matmul(pairs, resid=None, out_dtype=F32, tm=1024, tn=1024, name="matmul"):
    m = pairs[0][0].shape[0]
    n = pairs[0][1].shape[1]
    tn = min(tn, n)
    in_specs, args = [], []
    for a, b in pairs:
        k = a.shape[1]
        in_specs += [pl.BlockSpec((tm, k), lambda i, j: (i, 0)),
                     pl.BlockSpec((k, tn), lambda i, j: (0, j))]
        args += [a, b]
    if resid is not None:
        in_specs.append(pl.BlockSpec((tm, tn), lambda i, j: (i, j)))
        args.append(resid)
    return pl.pallas_call(
        functools.partial(_mm_kernel, n_pairs=len(pairs), has_resid=resid is not None),
        grid=(m // tm, n // tn),
        in_specs=in_specs,
        out_specs=pl.BlockSpec((tm, tn), lambda i, j: (i, j)),
        out_shape=jax.ShapeDtypeStruct((m, n), out_dtype),
        compiler_params=_params("arbitrary", "arbitrary"),
        name=name,
    )(*args)


def _ffn_up_kernel(h_ref, wg_ref, wu_ref, o_ref):
    h = h_ref[...]
    g = jnp.dot(h, wg_ref[...], preferred_element_type=F32)
    u = jnp.dot(h, wu_ref[...], preferred_element_type=F32)
    o_ref[...] = (_silu(g) * u).astype(o_ref.dtype)


def ffn_up(h, wg, wu, tm=1024, tn=512):
    m, k = h.shape
    n = wg.shape[1]
    return pl.pallas_call(
        _ffn_up_kernel,
        grid=(m // tm, n // tn),
        in_specs=[pl.BlockSpec((tm, k), lambda i, j: (i, 0)),
                  pl.BlockSpec((k, tn), lambda i, j: (0, j)),
                  pl.BlockSpec((k, tn), lambda i, j: (0, j))],
        out_specs=pl.BlockSpec((tm, tn), lambda i, j: (i, j)),
        out_shape=jax.ShapeDtypeStruct((m, n), BF16),
        compiler_params=_params("arbitrary", "arbitrary"),
        name="ffn_up",
    )(h, wg, wu)


def _rope_kernel(pos_ref, inv_ref, cos_ref, sin_ref):
    ang = pos_ref[...].astype(F32) * inv_ref[...]
    cos_ref[...] = jnp.cos(ang)
    sin_ref[...] = jnp.sin(ang)


def rope_tables(positions, tm=1024):
    s = positions.shape[0]
    tm = min(tm, s)
    half = RET_D // 2
    inv_freq = (ROPE_BASE ** (-np.arange(half, dtype=np.float32) / half)).astype(np.float32)
    return pl.pallas_call(
        _rope_kernel,
        grid=(s // tm,),
        in_specs=[pl.BlockSpec((tm, 1), lambda i: (i, 0)),
                  pl.BlockSpec((1, half), lambda i: (0, 0))],
        out_specs=[pl.BlockSpec((tm, half), lambda i: (i, 0))] * 2,
        out_shape=[jax.ShapeDtypeStruct((s, half), F32)] * 2,
        compiler_params=_params("arbitrary"),
        name="rope_tables",
    )(positions.reshape(s, 1), jnp.asarray(inv_freq).reshape(1, half))


def _ret_kernel(lg_ref, q_ref, k_ref, v_ref, g_ref, cos_ref, sin_ref, gain_ref, o_ref,
                state, decay, qdec, kdec):
    c = RET_CHUNK
    lg = lg_ref[pl.program_id(0)]

    @pl.when(pl.program_id(1) == 0)
    def _():
        state[...] = jnp.zeros_like(state)
        i = lax.broadcasted_iota(jnp.int32, (c, c), 0)
        j = lax.broadcasted_iota(jnp.int32, (c, c), 1)
        diff = (i - j).astype(F32)
        decay[...] = jnp.where(diff >= 0, jnp.exp(lg * jnp.maximum(diff, 0.0)), 0.0)
        r = lax.broadcasted_iota(jnp.int32, (c, RET_D), 0).astype(F32)
        qdec[...] = jnp.exp(lg * (r + 1.0))
        kdec[...] = jnp.exp(lg * (c - 1.0 - r))

    cos = cos_ref[...]
    sin = sin_ref[...]
    half = RET_D // 2

    def rot(t):
        t1, t2 = t[:, :half], t[:, half:]
        return jnp.concatenate([t1 * cos - t2 * sin, t1 * sin + t2 * cos], axis=1)

    q = rot(q_ref[...])
    k = rot(k_ref[...]) * (RET_D ** -0.5)
    vb = v_ref[...].astype(BF16)
    qb = q.astype(BF16)
    scores = lax.dot_general(qb, k.astype(BF16), NT_DIMS, preferred_element_type=F32) * decay[...]
    st = state[...]
    o = (jnp.dot(scores.astype(BF16), vb, preferred_element_type=F32)
         + jnp.dot(qb, st.astype(BF16), preferred_element_type=F32) * qdec[...])
    cdec = jnp.exp(jnp.zeros((1, RET_D), F32) + lg * c)
    kd = (k * kdec[...]).astype(BF16)
    state[...] = st * cdec + lax.dot_general(kd, vb, TN_DIMS, preferred_element_type=F32)

    mu = jnp.mean(o, axis=-1, keepdims=True)
    oc = o - mu
    var = jnp.mean(oc * oc, axis=-1, keepdims=True)
    y = oc * lax.rsqrt(var + EPS) * gain_ref[...]
    o_ref[...] = (_silu(g_ref[...]) * y).astype(o_ref.dtype)


def retention_heads(proj, cos, sin, gain):
    s = proj.shape[0]
    c = RET_CHUNK
    log_gamma = np.log1p(-np.exp2(-5.0 - np.arange(RET_HEADS, dtype=np.float32))).astype(np.float32)
    col = lambda base: (lambda h, n, lg: (n, base + h))
    grid_spec = pltpu.PrefetchScalarGridSpec(
        num_scalar_prefetch=1,
        grid=(RET_HEADS, s // c),
        in_specs=[pl.BlockSpec((c, RET_D), col(0)),
                  pl.BlockSpec((c, RET_D), col(RET_HEADS)),
                  pl.BlockSpec((c, RET_D), col(2 * RET_HEADS)),
                  pl.BlockSpec((c, RET_D), col(3 * RET_HEADS)),
                  pl.BlockSpec((c, RET_D // 2), lambda h, n, lg: (n, 0)),
                  pl.BlockSpec((c, RET_D // 2), lambda h, n, lg: (n, 0)),
                  pl.BlockSpec((1, RET_D), lambda h, n, lg: (0, h))],
        out_specs=pl.BlockSpec((c, RET_D), lambda h, n, lg: (n, h)),
        scratch_shapes=[pltpu.VMEM((RET_D, RET_D), F32),
                        pltpu.VMEM((c, c), F32),
                        pltpu.VMEM((c, RET_D), F32),
                        pltpu.VMEM((c, RET_D), F32)],
    )
    return pl.pallas_call(
        _ret_kernel,
        grid_spec=grid_spec,
        out_shape=jax.ShapeDtypeStruct((s, RET_WIDTH), BF16),
        compiler_params=_params("arbitrary", "arbitrary"),
        name="retention",
    )(jnp.asarray(log_gamma), proj, proj, proj, proj, cos, sin, gain.reshape(1, RET_WIDTH))


def _split3(x):
    hi = x.astype(BF16)
    r1 = x - hi.astype(F32)
    mid = r1.astype(BF16)
    lo = (r1 - mid.astype(F32)).astype(BF16)
    return hi, mid, lo


def _gla_kernel(q_ref, k_ref, v_ref, g_ref, a_ref, wup_ref, b_ref, gain_ref, o_ref,
                state_t, pair_xor, tri, q_s, k_s, cum_s, diag_s):
    c = GLA_CHUNK

    @pl.when(pl.program_id(1) == 0)
    def _():
        state_t[...] = jnp.zeros_like(state_t)
        i = lax.broadcasted_iota(jnp.int32, (c, c), 0)
        j = lax.broadcasted_iota(jnp.int32, (c, c), 1)
        tri[...] = jnp.where(i >= j, 1.0, 0.0).astype(BF16)
        pair_xor[...] = jnp.where(i > j, jnp.bitwise_xor(i, j), 0)

    a_hi, a_mid, a_lo = _split3(a_ref[...])
    w_hi, w_mid, w_lo = _split3(wup_ref[...])
    dotf = lambda x, y: jnp.dot(x, y, preferred_element_type=F32)
    z = (dotf(a_hi, w_hi) + (dotf(a_hi, w_mid) + dotf(a_mid, w_hi))
         + (dotf(a_hi, w_lo) + dotf(a_lo, w_hi) + dotf(a_mid, w_mid))) + b_ref[...]
    la = (jnp.minimum(z, 0.0) - jnp.log1p(jnp.exp(-jnp.abs(z)))) * (1.0 / GLA_GATE_TAU)
    l_hi, l_mid, l_lo = _split3(la)
    t = tri[...]
    cum = dotf(t, l_hi) + dotf(t, l_mid) + dotf(t, l_lo)

    qs = q_ref[...] * (GLA_DK ** -0.5)
    k = k_ref[...]
    vb = v_ref[...].astype(BF16)
    st = state_t[...]

    o = lax.dot_general((qs * jnp.exp(cum)).astype(BF16), st.astype(BF16), NT_DIMS,
                        preferred_element_type=F32)

    px = pair_xor[...]
    attn = jnp.zeros((c, c), F32)
    for p in GLA_LEVELS:
        half = 1 << p
        cum3 = cum.reshape(c // (2 * half), 2 * half, GLA_DK)
        e = jnp.exp(-jnp.abs(cum3 - cum3[:, half - 1:half, :])).reshape(c, GLA_DK)
        part = lax.dot_general((qs * e).astype(BF16), (k * e).astype(BF16), NT_DIMS,
                               preferred_element_type=F32)
        attn = jnp.where(lax.shift_right_logical(px, p) == 1, part, attn)
    o = o + jnp.dot(attn.astype(BF16), vb, preferred_element_type=F32)

    q_s[...] = qs
    k_s[...] = k
    cum_s[...] = cum
    rows = lax.broadcasted_iota(jnp.int32, (SUBLANES, 1), 0)

    def diag_block(bi, carry):
        r = pl.multiple_of(bi * SUBLANES, SUBLANES)
        q8 = q_s[pl.ds(r, SUBLANES), :]
        k8 = k_s[pl.ds(r, SUBLANES), :]
        c8 = cum_s[pl.ds(r, SUBLANES), :]
        v8 = v_ref[pl.ds(r, SUBLANES), :]
        acc = jnp.zeros((SUBLANES, GLA_DV), F32)
        for m in range(SUBLANES):
            w = jnp.exp(jnp.minimum(c8 - c8[m:m + 1, :], 0.0))
            a = jnp.sum(q8 * w * k8[m:m + 1, :], axis=-1, keepdims=True)
            a = jnp.where(rows >= m, a, 0.0)
            acc = acc + a * v8[m:m + 1, :]
        diag_s[pl.ds(r, SUBLANES), :] = acc
        return carry

    lax.fori_loop(0, c // SUBLANES, diag_block, 0)
    o = o + diag_s[...]

    last = cum[c - 1:c, :]
    kd = (k * jnp.exp(last - cum)).astype(BF16)
    state_t[...] = st * jnp.exp(last) + lax.dot_general(vb, kd, TN_DIMS, preferred_element_type=F32)

    ms = jnp.mean(o * o, axis=-1, keepdims=True)
    y = o * lax.rsqrt(ms + EPS) * gain_ref[...]
    o_ref[...] = (_silu(g_ref[...]) * y).astype(o_ref.dtype)


def gla_heads(proj, ga, w_up, b, gain):
    s = proj.shape[0]
    c = GLA_CHUNK
    qk_base = 4 * RET_WIDTH // GLA_DK
    v_base = (4 * RET_WIDTH + 2 * GLA_KEY_WIDTH) // GLA_DV
    w_up_pad = jnp.zeros((GATE_PAD, GLA_KEY_WIDTH), F32).at[:GLA_GATE_RANK].set(w_up)
    return pl.pallas_call(
        _gla_kernel,
        grid=(GLA_HEADS, s // c),
        in_specs=[pl.BlockSpec((c, GLA_DK), lambda h, n: (n, qk_base + h)),
                  pl.BlockSpec((c, GLA_DK), lambda h, n: (n, qk_base + GLA_HEADS + h)),
                  pl.BlockSpec((c, GLA_DV), lambda h, n: (n, v_base + h)),
                  pl.BlockSpec((c, GLA_DV), lambda h, n: (n, v_base + GLA_HEADS + h)),
                  pl.BlockSpec((c, GATE_PAD), lambda h, n: (n, 0)),
                  pl.BlockSpec((GATE_PAD, GLA_DK), lambda h, n: (0, h)),
                  pl.BlockSpec((1, GLA_DK), lambda h, n: (0, h)),
                  pl.BlockSpec((1, GLA_DV), lambda h, n: (0, h))],
        out_specs=pl.BlockSpec((c, GLA_DV), lambda h, n: (n, h)),
        out_shape=jax.ShapeDtypeStruct((s, GLA_WIDTH), BF16),
        scratch_shapes=[pltpu.VMEM((GLA_DV, GLA_DK), F32),
                        pltpu.VMEM((c, c), jnp.int32),
                        pltpu.VMEM((c, c), BF16),
                        pltpu.VMEM((c, GLA_DK), F32),
                        pltpu.VMEM((c, GLA_DK), F32),
                        pltpu.VMEM((c, GLA_DK), F32),
                        pltpu.VMEM((c, GLA_DV), F32)],
        compiler_params=_params("arbitrary", "arbitrary"),
        name="gla",
    )(proj, proj, proj, proj, ga, w_up_pad, b.reshape(1, GLA_KEY_WIDTH), gain.reshape(1, GLA_WIDTH))


def kernel(x, positions, mix_norm, w_in, gla_w_up, gla_b, ret_gain, gla_gain, w_out, ffn_norm,
           w_gate, w_up, w_down, final_norm):
    b, s, d = x.shape
    x = x.reshape(b * s, d)
    cos, sin = rope_tables(positions.reshape(b * s))
    ffn_extra = FFN_PAD - FFN_HIDDEN
    for l in range(DEPTH):
        w_main = w_in[l, :, :MAIN_WIDTH].astype(BF16)
        w_gate_lr = jnp.pad(w_in[l, :, MAIN_WIDTH:].astype(BF16), ((0, 0), (0, GATE_PAD - GLA_GATE_RANK)))
        w_o_ret = w_out[l, :RET_WIDTH].astype(BF16)
        w_o_gla = w_out[l, RET_WIDTH:].astype(BF16)
        wg = jnp.pad(w_gate[l].astype(BF16), ((0, 0), (0, ffn_extra)))
        wu = jnp.pad(w_up[l].astype(BF16), ((0, 0), (0, ffn_extra)))
        wd = jnp.pad(w_down[l].astype(BF16), ((0, ffn_extra), (0, 0)))

        h = rmsnorm(x, mix_norm[l], BF16)
        proj = matmul([(h, w_main)], name="in_proj")
        ga = matmul([(h, w_gate_lr)], name="gate_proj")
        r_out = retention_heads(proj, cos, sin, ret_gain[l])
        g_out = gla_heads(proj, ga, gla_w_up[l], gla_b[l], gla_gain[l])
        x = matmul([(r_out, w_o_ret), (g_out, w_o_gla)], resid=x, name="out_proj")
        h = rmsnorm(x, ffn_norm[l], BF16)
        hid = ffn_up(h, wg, wu)
        x = matmul([(hid, wd)], resid=x, tm=512, tn=512, name="ffn_down")
    return rmsnorm(x, final_norm, F32).reshape(b, s, d)
```

```python
import functools

import numpy as np
import jax
import jax.numpy as jnp
from jax import lax
from jax.experimental import pallas as pl
from jax.experimental.pallas import tpu as pltpu

F32 = jnp.float32
BF16 = jnp.bfloat16

D_MODEL = 4096
DEPTH = 2
RET_HEADS = 8
RET_D = 256
RET_WIDTH = RET_HEADS * RET_D
GLA_HEADS = 4
GLA_DK = 256
GLA_DV = 512
GLA_KEY_WIDTH = GLA_HEADS * GLA_DK
GLA_WIDTH = GLA_HEADS * GLA_DV
GLA_GATE_RANK = 16
GLA_GATE_TAU = 16.0
FFN_HIDDEN = 11008
ROPE_BASE = 10000.0
EPS = 1e-6

MAIN_WIDTH = 4 * RET_WIDTH + 2 * GLA_KEY_WIDTH + 2 * GLA_WIDTH
LANES = 128
SUBLANES = 8
GATE_PAD = LANES

RET_CHUNK = 256
GLA_CHUNK = 256
GLA_LEVELS = (7, 6, 5, 4, 3)
GLA_DIAG_UNROLL = 4

VMEM_LIMIT = 56 * 1024 * 1024

NT_DIMS = (((1,), (1,)), ((), ()))
TN_DIMS = (((0,), (0,)), ((), ()))


def _params(*sem):
    return pltpu.CompilerParams(dimension_semantics=sem, vmem_limit_bytes=VMEM_LIMIT)


def _silu(x):
    return x * (1.0 / (1.0 + jnp.exp(-x)))


def _rmsnorm_kernel(x_ref, g_ref, o_ref):
    x = x_ref[...]
    ms = jnp.mean(x * x, axis=-1, keepdims=True)
    o_ref[...] = (x * lax.rsqrt(ms + EPS) * g_ref[...]).astype(o_ref.dtype)


def rmsnorm(x, gain, out_dtype, tm=256):
    s, d = x.shape
    return pl.pallas_call(
        _rmsnorm_kernel,
        grid=(s // tm,),
        in_specs=[pl.BlockSpec((tm, d), lambda i: (i, 0)),
                  pl.BlockSpec((1, d), lambda i: (0, 0))],
        out_specs=pl.BlockSpec((tm, d), lambda i: (i, 0)),
        out_shape=jax.ShapeDtypeStruct((s, d), out_dtype),
        compiler_params=_params("arbitrary"),
        name="rmsnorm",
    )(x, gain.reshape(1, d))


def _mm_kernel(*refs, n_pairs, has_resid, valid_cols):
    o_ref = refs[-1]
    acc = None
    for p in range(n_pairs):
        w = refs[2 * p + 1][...]
        if valid_cols is not None:
            col = lax.broadcasted_iota(jnp.int32, w.shape, 1)
            w = jnp.where(col < valid_cols, w, 0.0)
        part = jnp.dot(refs[2 * p][...], w.astype(BF16), preferred_element_type=F32)
        acc = part if acc is None else acc + part
    if has_resid:
        acc = acc + refs[2 * n_pairs][...]
    o_ref[...] = acc.astype(o_ref.dtype)


def matmul(pairs, n, resid=None, tm=2048, tn=256, valid_cols=None, name="matmul"):
    m = pairs[0][0].shape[0]
    in_specs, args = [], []
    for a, w, layer, rb, cb0 in pairs:
        k = a.shape[1]
        in_specs.append(pl.BlockSpec((tm, k), lambda i, j: (i, 0)))
        if w.ndim == 3:
            in_specs.append(pl.BlockSpec((None, k, tn), lambda i, j, l=layer, r=rb, c=cb0: (l, r, c + j)))
        else:
            in_specs.append(pl.BlockSpec((k, tn), lambda i, j, r=rb, c=cb0: (r, c + j)))
        args += [a, w]
    if resid is not None:
        in_specs.append(pl.BlockSpec((tm, tn), lambda i, j: (i, j)))
        args.append(resid)
    return pl.pallas_call(
        functools.partial(_mm_kernel, n_pairs=len(pairs), has_resid=resid is not None,
                          valid_cols=valid_cols),
        grid=(m // tm, n // tn),
        in_specs=in_specs,
        out_specs=pl.BlockSpec((tm, tn), lambda i, j: (i, j)),
        out_shape=jax.ShapeDtypeStruct((m, n), F32),
        compiler_params=_params("arbitrary", "arbitrary"),
        name=name,
    )(*args)


def _ffn_up_kernel(h_ref, wg_ref, wu_ref, o_ref):
    h = h_ref[...]
    g = jnp.dot(h, wg_ref[...].astype(BF16), preferred_element_type=F32)
    u = jnp.dot(h, wu_ref[...].astype(BF16), preferred_element_type=F32)
    o_ref[...] = (_silu(g) * u).astype(o_ref.dtype)


def ffn_up(h, w_gate, w_up, layer, tm=2048, tn=256):
    m, k = h.shape
    n = w_gate.shape[2]
    w_spec = pl.BlockSpec((None, k, tn), lambda i, j: (layer, 0, j))
    return pl.pallas_call(
        _ffn_up_kernel,
        grid=(m // tm, n // tn),
        in_specs=[pl.BlockSpec((tm, k), lambda i, j: (i, 0)), w_spec, w_spec],
        out_specs=pl.BlockSpec((tm, tn), lambda i, j: (i, j)),
        out_shape=jax.ShapeDtypeStruct((m, n), BF16),
        compiler_params=_params("arbitrary", "arbitrary"),
        name="ffn_up",
    )(h, w_gate, w_up)


def _cast_kernel(x_ref, o_ref):
    o_ref[...] = x_ref[...].astype(o_ref.dtype)


def cast_rows_bf16(w, layer, tr=688):
    _, r, c = w.shape
    return pl.pallas_call(
        _cast_kernel,
        grid=(r // tr,),
        in_specs=[pl.BlockSpec((None, tr, c), lambda i: (layer, i, 0))],
        out_specs=pl.BlockSpec((tr, c), lambda i: (i, 0)),
        out_shape=jax.ShapeDtypeStruct((r, c), BF16),
        compiler_params=_params("arbitrary"),
        name="cast_w_down",
    )(w)


def _rope_kernel(pos_ref, inv_ref, cos_ref, sin_ref):
    ang = pos_ref[...].astype(F32) * inv_ref[...]
    cos_ref[...] = jnp.cos(ang)
    sin_ref[...] = jnp.sin(ang)


def rope_tables(positions, tm=1024):
    s = positions.shape[0]
    tm = min(tm, s)
    half = RET_D // 2
    inv_freq = (ROPE_BASE ** (-np.arange(half, dtype=np.float32) / half)).astype(np.float32)
    return pl.pallas_call(
        _rope_kernel,
        grid=(s // tm,),
        in_specs=[pl.BlockSpec((tm, 1), lambda i: (i, 0)),
                  pl.BlockSpec((1, half), lambda i: (0, 0))],
        out_specs=[pl.BlockSpec((tm, half), lambda i: (i, 0))] * 2,
        out_shape=[jax.ShapeDtypeStruct((s, half), F32)] * 2,
        compiler_params=_params("arbitrary"),
        name="rope_tables",
    )(positions.reshape(s, 1), jnp.asarray(inv_freq).reshape(1, half))


def _ret_kernel(lg_ref, q_ref, k_ref, v_ref, g_ref, cos_ref, sin_ref, gain_ref, o_ref,
                state, decay, qdec, kdec):
    c = RET_CHUNK
    lg = lg_ref[pl.program_id(0)]

    @pl.when(pl.program_id(1) == 0)
    def _():
        state[...] = jnp.zeros_like(state)
        i = lax.broadcasted_iota(jnp.int32, (c, c), 0)
        j = lax.broadcasted_iota(jnp.int32, (c, c), 1)
        diff = (i - j).astype(F32)
        decay[...] = jnp.where(diff >= 0, jnp.exp(lg * jnp.maximum(diff, 0.0)), 0.0)
        r = lax.broadcasted_iota(jnp.int32, (c, RET_D), 0).astype(F32)
        qdec[...] = jnp.exp(lg * (r + 1.0))
        kdec[...] = jnp.exp(lg * (c - 1.0 - r))

    cos = cos_ref[...]
    sin = sin_ref[...]
    half = RET_D // 2

    def rot(t):
        t1, t2 = t[:, :half], t[:, half:]
        return jnp.concatenate([t1 * cos - t2 * sin, t1 * sin + t2 * cos], axis=1)

    q = rot(q_ref[...])
    k = rot(k_ref[...]) * (RET_D ** -0.5)
    vb = v_ref[...].astype(BF16)
    qb = q.astype(BF16)
    scores = lax.dot_general(qb, k.astype(BF16), NT_DIMS, preferred_element_type=F32) * decay[...]
    st = state[...]
    o = (jnp.dot(scores.astype(BF16), vb, preferred_element_type=F32)
         + jnp.dot(qb, st.astype(BF16), preferred_element_type=F32) * qdec[...])
    cdec = jnp.exp(jnp.zeros((1, RET_D), F32) + lg * c)
    kd = (k * kdec[...]).astype(BF16)
    state[...] = st * cdec + lax.dot_general(kd, vb, TN_DIMS, preferred_element_type=F32)

    mu = jnp.mean(o, axis=-1, keepdims=True)
    oc = o - mu
    var = jnp.mean(oc * oc, axis=-1, keepdims=True)
    y = oc * lax.rsqrt(var + EPS) * gain_ref[...]
    o_ref[...] = (_silu(g_ref[...]) * y).astype(o_ref.dtype)


def retention_heads(proj, cos, sin, gain):
    s = proj.shape[0]
    c = RET_CHUNK
    log_gamma = np.log1p(-np.exp2(-5.0 - np.arange(RET_HEADS, dtype=np.float32))).astype(np.float32)
    col = lambda base: (lambda h, n, lg: (n, base + h))
    grid_spec = pltpu.PrefetchScalarGridSpec(
        num_scalar_prefetch=1,
        grid=(RET_HEADS, s // c),
        in_specs=[pl.BlockSpec((c, RET_D), col(0)),
                  pl.BlockSpec((c, RET_D), col(RET_HEADS)),
                  pl.BlockSpec((c, RET_D), col(2 * RET_HEADS)),
                  pl.BlockSpec((c, RET_D), col(3 * RET_HEADS)),
                  pl.BlockSpec((c, RET_D // 2), lambda h, n, lg: (n, 0)),
                  pl.BlockSpec((c, RET_D // 2), lambda h, n, lg: (n, 0)),
                  pl.BlockSpec((1, RET_D), lambda h, n, lg: (0, h))],
        out_specs=pl.BlockSpec((c, RET_D), lambda h, n, lg: (n, h)),
        scratch_shapes=[pltpu.VMEM((RET_D, RET_D), F32),
                        pltpu.VMEM((c, c), F32),
                        pltpu.VMEM((c, RET_D), F32),
                        pltpu.VMEM((c, RET_D), F32)],
    )
    return pl.pallas_call(
        _ret_kernel,
        grid_spec=grid_spec,
        out_shape=jax.ShapeDtypeStruct((s, RET_WIDTH), BF16),
        compiler_params=_params("arbitrary", "arbitrary"),
        name="retention",
    )(jnp.asarray(log_gamma), proj, proj, proj, proj, cos, sin, gain.reshape(1, RET_WIDTH))


def _split3(x):
    hi = x.astype(BF16)
    r1 = x - hi.astype(F32)
    mid = r1.astype(BF16)
    lo = (r1 - mid.astype(F32)).astype(BF16)
    return hi, mid, lo


def _gla_kernel(q_ref, k_ref, v_ref, g_ref, a_ref, wup_ref, b_ref, gain_ref, o_ref,
                state_t, pair_level, tri, q_s, k_s, cum_s, attn_s):
    c = GLA_CHUNK

    @pl.when(pl.program_id(1) == 0)
    def _():
        state_t[...] = jnp.zeros_like(state_t)
        i = lax.broadcasted_iota(jnp.int32, (c, c), 0)
        j = lax.broadcasted_iota(jnp.int32, (c, c), 1)
        tri[...] = jnp.where(i >= j, 1.0, 0.0).astype(BF16)
        x = jnp.where(i > j, jnp.bitwise_xor(i, j), 0)
        lvl = jnp.full((c, c), GLA_LEVELS[-1] - 1, jnp.int32)
        for p in GLA_LEVELS:
            lvl = lvl + jnp.where(x >= (1 << p), 1, 0)
        pair_level[...] = lvl

    a_hi, a_mid, a_lo = _split3(a_ref[...])
    w_hi, w_mid, w_lo = _split3(wup_ref[...])
    dotf = lambda x, y: jnp.dot(x, y, preferred_element_type=F32)
    z = (dotf(a_hi, w_hi) + (dotf(a_hi, w_mid) + dotf(a_mid, w_hi))
         + (dotf(a_hi, w_lo) + dotf(a_lo, w_hi) + dotf(a_mid, w_mid))) + b_ref[...]
    la = (jnp.minimum(z, 0.0) - jnp.log1p(jnp.exp(-jnp.abs(z)))) * (1.0 / GLA_GATE_TAU)
    l_hi, l_mid, l_lo = _split3(la)
    t = tri[...]
    cum = dotf(t, l_hi) + dotf(t, l_mid) + dotf(t, l_lo)

    qs = q_ref[...] * (GLA_DK ** -0.5)
    k = k_ref[...]
    vb = v_ref[...].astype(BF16)
    st = state_t[...]

    o = lax.dot_general((qs * jnp.exp(cum)).astype(BF16), st.astype(BF16), NT_DIMS,
                        preferred_element_type=F32)

    lvl = pair_level[...]
    attn = jnp.zeros((c, c), F32)
    for p in GLA_LEVELS:
        half = 1 << p
        cum3 = cum.reshape(c // (2 * half), 2 * half, GLA_DK)
        e = jnp.exp(-jnp.abs(cum3 - cum3[:, half - 1:half, :])).reshape(c, GLA_DK)
        part = lax.dot_general((qs * e).astype(BF16), (k * e).astype(BF16), NT_DIMS,
                               preferred_element_type=F32)
        attn = jnp.where(lvl == p, part, attn)

    q_s[...] = qs
    k_s[...] = k
    cum_s[...] = cum
    attn_s[...] = attn
    rows = lax.broadcasted_iota(jnp.int32, (SUBLANES, 1), 0)
    lane = lax.broadcasted_iota(jnp.int32, (SUBLANES, c), 1)

    def diag_blocks(t_i, carry):
        for u in range(GLA_DIAG_UNROLL):
            r = pl.multiple_of((t_i * GLA_DIAG_UNROLL + u) * SUBLANES, SUBLANES)
            q8 = q_s[pl.ds(r, SUBLANES), :]
            k8 = k_s[pl.ds(r, SUBLANES), :]
            c8 = cum_s[pl.ds(r, SUBLANES), :]
            slab = attn_s[pl.ds(r, SUBLANES), :]
            for m in range(SUBLANES):
                w = jnp.exp(jnp.minimum(c8 - c8[m:m + 1, :], 0.0))
                a = jnp.sum(q8 * w * k8[m:m + 1, :], axis=-1, keepdims=True)
                a = jnp.where(rows >= m, a, 0.0)
                slab = jnp.where(lane == r + m, a, slab)
            attn_s[pl.ds(r, SUBLANES), :] = slab
        return carry

    lax.fori_loop(0, c // (SUBLANES * GLA_DIAG_UNROLL), diag_blocks, 0)
    o = o + jnp.dot(attn_s[...].astype(BF16), vb, preferred_element_type=F32)

    last = cum[c - 1:c, :]
    kd = (k * jnp.exp(last - cum)).astype(BF16)
    state_t[...] = st * jnp.exp(last) + lax.dot_general(vb, kd, TN_DIMS, preferred_element_type=F32)

    ms = jnp.mean(o * o, axis=-1, keepdims=True)
    y = o * lax.rsqrt(ms + EPS) * gain_ref[...]
    o_ref[...] = (_silu(g_ref[...]) * y).astype(o_ref.dtype)


def gla_heads(proj, ga, w_up, b, gain):
    s = proj.shape[0]
    c = GLA_CHUNK
    qk_base = 4 * RET_WIDTH // GLA_DK
    v_base = (4 * RET_WIDTH + 2 * GLA_KEY_WIDTH) // GLA_DV
    w_up_pad = jnp.zeros((GATE_PAD, GLA_KEY_WIDTH), F32).at[:GLA_GATE_RANK].set(w_up)
    return pl.pallas_call(
        _gla_kernel,
        grid=(GLA_HEADS, s // c),
        in_specs=[pl.BlockSpec((c, GLA_DK), lambda h, n: (n, qk_base + h)),
                  pl.BlockSpec((c, GLA_DK), lambda h, n: (n, qk_base + GLA_HEADS + h)),
                  pl.BlockSpec((c, GLA_DV), lambda h, n: (n, v_base + h)),
                  pl.BlockSpec((c, GLA_DV), lambda h, n: (n, v_base + GLA_HEADS + h)),
                  pl.BlockSpec((c, GATE_PAD), lambda h, n: (n, 0)),
                  pl.BlockSpec((GATE_PAD, GLA_DK), lambda h, n: (0, h)),
                  pl.BlockSpec((1, GLA_DK), lambda h, n: (0, h)),
                  pl.BlockSpec((1, GLA_DV), lambda h, n: (0, h))],
        out_specs=pl.BlockSpec((c, GLA_DV), lambda h, n: (n, h)),
        out_shape=jax.ShapeDtypeStruct((s, GLA_WIDTH), BF16),
        scratch_shapes=[pltpu.VMEM((GLA_DV, GLA_DK), F32),
                        pltpu.VMEM((c, c), jnp.int32),
                        pltpu.VMEM((c, c), BF16),
                        pltpu.VMEM((c, GLA_DK), F32),
                        pltpu.VMEM((c, GLA_DK), F32),
                        pltpu.VMEM((c, GLA_DK), F32),
                        pltpu.VMEM((c, c), F32)],
        compiler_params=_params("arbitrary", "arbitrary"),
        name="gla",
    )(proj, proj, proj, proj, ga, w_up_pad, b.reshape(1, GLA_KEY_WIDTH), gain.reshape(1, GLA_WIDTH))


def kernel(x, positions, mix_norm, w_in, gla_w_up, gla_b, ret_gain, gla_gain, w_out, ffn_norm,
           w_gate, w_up, w_down, final_norm):
    b, s, d = x.shape
    x = x.reshape(b * s, d)
    cos, sin = rope_tables(positions.reshape(b * s))
    for l in range(DEPTH):
        h = rmsnorm(x, mix_norm[l], BF16)
        proj = matmul([(h, w_in, l, 0, 0)], MAIN_WIDTH, name="in_proj")
        ga = matmul([(h, w_in, l, 0, MAIN_WIDTH // GATE_PAD)], GATE_PAD, tn=GATE_PAD,
                    valid_cols=GLA_GATE_RANK, name="gate_proj")
        r_out = retention_heads(proj, cos, sin, ret_gain[l])
        g_out = gla_heads(proj, ga, gla_w_up[l], gla_b[l], gla_gain[l])
        x = matmul([(r_out, w_out, l, 0, 0), (g_out, w_out, l, 1, 0)], d, resid=x,
                   tm=1024, tn=512, name="out_proj")
        h = rmsnorm(x, ffn_norm[l], BF16)
        hid = ffn_up(h, w_gate, w_up, l)
        wd = cast_rows_bf16(w_down, l)
        x = matmul([(hid, wd, None, 0, 0)], d, resid=x, tm=512, tn=512, name="ffn_down")
    return rmsnorm(x, final_norm, F32).reshape(b, s, d)
```

```python
import functools

import numpy as np
import jax
import jax.numpy as jnp
from jax import lax
from jax.experimental import pallas as pl
from jax.experimental.pallas import tpu as pltpu

F32 = jnp.float32
BF16 = jnp.bfloat16

D_MODEL = 4096
DEPTH = 2
RET_HEADS = 8
RET_D = 256
RET_WIDTH = RET_HEADS * RET_D
GLA_HEADS = 4
GLA_DK = 256
GLA_DV = 512
GLA_KEY_WIDTH = GLA_HEADS * GLA_DK
GLA_WIDTH = GLA_HEADS * GLA_DV
GLA_GATE_RANK = 16
GLA_GATE_TAU = 16.0
FFN_HIDDEN = 11008
ROPE_BASE = 10000.0
EPS = 1e-6

MAIN_WIDTH = 4 * RET_WIDTH + 2 * GLA_KEY_WIDTH + 2 * GLA_WIDTH
LANES = 128
SUBLANES = 8
GATE_PAD = LANES

RET_CHUNK = 256
RET_CHUNKS_PER_STEP = 4
GLA_CHUNK = 256
GLA_LEVELS = (7, 6, 5, 4, 3)

VMEM_LIMIT = 56 * 1024 * 1024

NT_DIMS = (((1,), (1,)), ((), ()))
TN_DIMS = (((0,), (0,)), ((), ()))


def _params(*sem):
    return pltpu.CompilerParams(dimension_semantics=sem, vmem_limit_bytes=VMEM_LIMIT)


def _silu(x):
    return x * (1.0 / (1.0 + jnp.exp(-x)))


def _rmsnorm_kernel(x_ref, g_ref, o_ref):
    x = x_ref[...]
    ms = jnp.mean(x * x, axis=-1, keepdims=True)
    o_ref[...] = (x * lax.rsqrt(ms + EPS) * g_ref[...]).astype(o_ref.dtype)


def rmsnorm(x, gain, out_dtype, tm=256):
    s, d = x.shape
    return pl.pallas_call(
        _rmsnorm_kernel,
        grid=(s // tm,),
        in_specs=[pl.BlockSpec((tm, d), lambda i: (i, 0)),
                  pl.BlockSpec((1, d), lambda i: (0, 0))],
        out_specs=pl.BlockSpec((tm, d), lambda i: (i, 0)),
        out_shape=jax.ShapeDtypeStruct((s, d), out_dtype),
        compiler_params=_params("arbitrary"),
        name="rmsnorm",
    )(x, gain.reshape(1, d))


def _mm_kernel(*refs, n_pairs, has_resid, valid_cols, w_transposed):
    o_ref = refs[-1]
    acc = None
    for p in range(n_pairs):
        w = refs[2 * p + 1][...]
        if valid_cols is not None:
            col = lax.broadcasted_iota(jnp.int32, w.shape, 0 if w_transposed else 1)
            w = jnp.where(col < valid_cols, w, 0.0)
        if w_transposed:
            part = lax.dot_general(refs[2 * p][...], w.astype(BF16), NT_DIMS, preferred_element_type=F32)
        else:
            part = jnp.dot(refs[2 * p][...], w.astype(BF16), preferred_element_type=F32)
        acc = part if acc is None else acc + part
    if has_resid:
        acc = acc + refs[2 * n_pairs][...]
    o_ref[...] = acc.astype(o_ref.dtype)


def matmul(pairs, n, resid=None, tm=2048, tn=256, valid_cols=None, w_transposed=False, name="matmul"):
    m = pairs[0][0].shape[0]
    in_specs, args = [], []
    for a, w, layer, rb, cb0 in pairs:
        k = a.shape[1]
        in_specs.append(pl.BlockSpec((tm, k), lambda i, j: (i, 0)))
        if w_transposed:
            in_specs.append(pl.BlockSpec((None, tn, k), lambda i, j, l=layer, r=rb, c=cb0: (l, c + j, r)))
        elif w.ndim == 3:
            in_specs.append(pl.BlockSpec((None, k, tn), lambda i, j, l=layer, r=rb, c=cb0: (l, r, c + j)))
        else:
            in_specs.append(pl.BlockSpec((k, tn), lambda i, j, r=rb, c=cb0: (r, c + j)))
        args += [a, w]
    if resid is not None:
        in_specs.append(pl.BlockSpec((tm, tn), lambda i, j: (i, j)))
        args.append(resid)
    return pl.pallas_call(
        functools.partial(_mm_kernel, n_pairs=len(pairs), has_resid=resid is not None,
                          valid_cols=valid_cols, w_transposed=w_transposed),
        grid=(m // tm, n // tn),
        in_specs=in_specs,
        out_specs=pl.BlockSpec((tm, tn), lambda i, j: (i, j)),
        out_shape=jax.ShapeDtypeStruct((m, n), F32),
        compiler_params=_params("arbitrary", "arbitrary"),
        name=name,
    )(*args)


def _ffn_up_kernel(h_ref, wg_ref, wu_ref, o_ref):
    h = h_ref[...]
    g = jnp.dot(h, wg_ref[...].astype(BF16), preferred_element_type=F32)
    u = jnp.dot(h, wu_ref[...].astype(BF16), preferred_element_type=F32)
    o_ref[...] = (_silu(g) * u).astype(o_ref.dtype)


def ffn_up(h, w_gate, w_up, layer, tm=2048, tn=256):
    m, k = h.shape
    n = w_gate.shape[2]
    w_spec = pl.BlockSpec((None, k, tn), lambda i, j: (layer, 0, j))
    return pl.pallas_call(
        _ffn_up_kernel,
        grid=(m // tm, n // tn),
        in_specs=[pl.BlockSpec((tm, k), lambda i, j: (i, 0)), w_spec, w_spec],
        out_specs=pl.BlockSpec((tm, tn), lambda i, j: (i, j)),
        out_shape=jax.ShapeDtypeStruct((m, n), BF16),
        compiler_params=_params("arbitrary", "arbitrary"),
        name="ffn_up",
    )(h, w_gate, w_up)


def _cast_kernel(x_ref, o_ref):
    o_ref[...] = x_ref[...].astype(o_ref.dtype)


def cast_rows_bf16(w, layer, tr=688):
    _, r, c = w.shape
    return pl.pallas_call(
        _cast_kernel,
        grid=(r // tr,),
        in_specs=[pl.BlockSpec((None, tr, c), lambda i: (layer, i, 0))],
        out_specs=pl.BlockSpec((tr, c), lambda i: (i, 0)),
        out_shape=jax.ShapeDtypeStruct((r, c), BF16),
        compiler_params=_params("arbitrary"),
        name="cast_w_down",
    )(w)


def _rope_kernel(pos_ref, inv_ref, cos_ref, sin_ref):
    ang = pos_ref[...].astype(F32) * inv_ref[...]
    cos_ref[...] = jnp.cos(ang)
    sin_ref[...] = jnp.sin(ang)


def rope_tables(positions, tm=1024):
    s = positions.shape[0]
    tm = min(tm, s)
    half = RET_D // 2
    inv_freq = (ROPE_BASE ** (-np.arange(half, dtype=np.float32) / half)).astype(np.float32)
    return pl.pallas_call(
        _rope_kernel,
        grid=(s // tm,),
        in_specs=[pl.BlockSpec((tm, 1), lambda i: (i, 0)),
                  pl.BlockSpec((1, half), lambda i: (0, 0))],
        out_specs=[pl.BlockSpec((tm, half), lambda i: (i, 0))] * 2,
        out_shape=[jax.ShapeDtypeStruct((s, half), F32)] * 2,
        compiler_params=_params("arbitrary"),
        name="rope_tables",
    )(positions.reshape(s, 1), jnp.asarray(inv_freq).reshape(1, half))


def _ret_kernel(lg_ref, q_ref, k_ref, v_ref, g_ref, cos_ref, sin_ref, gain_ref, o_ref,
                state, decay, qdec, kdec):
    c = RET_CHUNK
    lg = lg_ref[pl.program_id(0)]

    @pl.when(pl.program_id(1) == 0)
    def _():
        state[...] = jnp.zeros_like(state)
        i = lax.broadcasted_iota(jnp.int32, (c, c), 0)
        j = lax.broadcasted_iota(jnp.int32, (c, c), 1)
        diff = (i - j).astype(F32)
        decay[...] = jnp.where(diff >= 0, jnp.exp(lg * jnp.maximum(diff, 0.0)), 0.0)
        r = lax.broadcasted_iota(jnp.int32, (c, RET_D), 0).astype(F32)
        qdec[...] = jnp.exp(lg * (r + 1.0))
        kdec[...] = jnp.exp(lg * (c - 1.0 - r))

    half = RET_D // 2
    cdec = jnp.exp(jnp.zeros((1, RET_D), F32) + lg * c)
    st = state[...]
    for ci in range(RET_CHUNKS_PER_STEP):
        rs = slice(ci * c, (ci + 1) * c)
        cos = cos_ref[rs, :]
        sin = sin_ref[rs, :]

        def rot(t):
            t1, t2 = t[:, :half], t[:, half:]
            return jnp.concatenate([t1 * cos - t2 * sin, t1 * sin + t2 * cos], axis=1)

        q = rot(q_ref[rs, :])
        k = rot(k_ref[rs, :]) * (RET_D ** -0.5)
        vb = v_ref[rs, :].astype(BF16)
        qb = q.astype(BF16)
        scores = lax.dot_general(qb, k.astype(BF16), NT_DIMS, preferred_element_type=F32) * decay[...]
        o = (jnp.dot(scores.astype(BF16), vb, preferred_element_type=F32)
             + jnp.dot(qb, st.astype(BF16), preferred_element_type=F32) * qdec[...])
        kd = (k * kdec[...]).astype(BF16)
        st = st * cdec + lax.dot_general(kd, vb, TN_DIMS, preferred_element_type=F32)

        mu = jnp.mean(o, axis=-1, keepdims=True)
        oc = o - mu
        var = jnp.mean(oc * oc, axis=-1, keepdims=True)
        y = oc * lax.rsqrt(var + EPS) * gain_ref[...]
        o_ref[rs, :] = (_silu(g_ref[rs, :]) * y).astype(o_ref.dtype)
    state[...] = st


def retention_heads(proj, cos, sin, gain):
    s = proj.shape[0]
    c = RET_CHUNK
    log_gamma = np.log1p(-np.exp2(-5.0 - np.arange(RET_HEADS, dtype=np.float32))).astype(np.float32)
    col = lambda base: (lambda h, n, lg: (n, base + h))
    rows = min(c * RET_CHUNKS_PER_STEP, s)
    grid_spec = pltpu.PrefetchScalarGridSpec(
        num_scalar_prefetch=1,
        grid=(RET_HEADS, s // rows),
        in_specs=[pl.BlockSpec((rows, RET_D), col(0)),
                  pl.BlockSpec((rows, RET_D), col(RET_HEADS)),
                  pl.BlockSpec((rows, RET_D), col(2 * RET_HEADS)),
                  pl.BlockSpec((rows, RET_D), col(3 * RET_HEADS)),
                  pl.BlockSpec((rows, RET_D // 2), lambda h, n, lg: (n, 0)),
                  pl.BlockSpec((rows, RET_D // 2), lambda h, n, lg: (n, 0)),
                  pl.BlockSpec((1, RET_D), lambda h, n, lg: (0, h))],
        out_specs=pl.BlockSpec((rows, RET_D), lambda h, n, lg: (n, h)),
        scratch_shapes=[pltpu.VMEM((RET_D, RET_D), F32),
                        pltpu.VMEM((c, c), F32),
                        pltpu.VMEM((c, RET_D), F32),
                        pltpu.VMEM((c, RET_D), F32)],
    )
    return pl.pallas_call(
        _ret_kernel,
        grid_spec=grid_spec,
        out_shape=jax.ShapeDtypeStruct((s, RET_WIDTH), BF16),
        compiler_params=_params("arbitrary", "arbitrary"),
        name="retention",
    )(jnp.asarray(log_gamma), proj, proj, proj, proj, cos, sin, gain.reshape(1, RET_WIDTH))


def _split3(x):
    hi = x.astype(BF16)
    r1 = x - hi.astype(F32)
    mid = r1.astype(BF16)
    lo = (r1 - mid.astype(F32)).astype(BF16)
    return hi, mid, lo


def _gla_kernel(q_ref, k_ref, v_ref, g_ref, a_ref, wup_ref, b_ref, gain_ref, o_ref,
                state_t, pair_level, tri, attn_s):
    c = GLA_CHUNK
    log2e = 1.4426950408889634

    @pl.when(pl.program_id(1) == 0)
    def _():
        state_t[...] = jnp.zeros_like(state_t)
        i = lax.broadcasted_iota(jnp.int32, (c, c), 0)
        j = lax.broadcasted_iota(jnp.int32, (c, c), 1)
        tri[...] = jnp.where(i >= j, 1.0, 0.0).astype(BF16)
        x = jnp.where(i > j, jnp.bitwise_xor(i, j), 0)
        lvl = jnp.full((c, c), GLA_LEVELS[-1] - 1, jnp.int32)
        for p in GLA_LEVELS:
            lvl = lvl + jnp.where(x >= (1 << p), 1, 0)
        pair_level[...] = lvl

    a_hi, a_mid, a_lo = _split3(a_ref[...])
    w_hi, w_mid, w_lo = _split3(wup_ref[...])
    dotf = lambda x, y: jnp.dot(x, y, preferred_element_type=F32)
    z = (dotf(a_hi, w_hi) + (dotf(a_hi, w_mid) + dotf(a_mid, w_hi))
         + (dotf(a_hi, w_lo) + dotf(a_lo, w_hi) + dotf(a_mid, w_mid))) + b_ref[...]
    la = (jnp.minimum(z, 0.0) - jnp.log1p(jnp.exp(-jnp.abs(z)))) * (log2e / GLA_GATE_TAU)
    l_hi, l_mid, l_lo = _split3(la)
    t = tri[...]
    cum = dotf(t, l_hi) + dotf(t, l_mid) + dotf(t, l_lo)

    qs = q_ref[...] * (GLA_DK ** -0.5)
    k = k_ref[...]
    vb = v_ref[...].astype(BF16)
    st = state_t[...]

    o = lax.dot_general((qs * jnp.exp2(cum)).astype(BF16), st.astype(BF16), NT_DIMS,
                        preferred_element_type=F32)

    attn_s[...] = jnp.zeros_like(attn_s)
    for p in GLA_LEVELS:
        half = 1 << p
        nb = c // (2 * half)
        q_rows, k_rows = [], []
        for blk in range(nb):
            up = slice(blk * 2 * half, blk * 2 * half + half)
            lo = slice(blk * 2 * half + half, (blk + 1) * 2 * half)
            bnd = cum[up.stop - 1:up.stop, :]
            e_lo = jnp.exp2(cum[lo] - bnd)
            k_rows += [k[up] * jnp.exp2(bnd - cum[up]), k[lo] * e_lo]
            q_rows.append(qs[lo] * e_lo)
        q_t = q_rows[0] if nb == 1 else jnp.concatenate(q_rows, axis=0)
        k_t = jnp.concatenate(k_rows, axis=0)
        part = lax.dot_general(q_t.astype(BF16), k_t.astype(BF16), NT_DIMS,
                               preferred_element_type=F32)
        for blk in range(nb):
            lo = slice(blk * 2 * half + half, (blk + 1) * 2 * half)
            attn_s[lo, :] = jnp.where(pair_level[lo, :] == p,
                                      part[blk * half:(blk + 1) * half, :], attn_s[lo, :])

    rows = lax.broadcasted_iota(jnp.int32, (SUBLANES, 1), 0)
    lane = lax.broadcasted_iota(jnp.int32, (SUBLANES, LANES), 1)
    for blk in range(c // SUBLANES):
        r = blk * SUBLANES
        rs = slice(r, r + SUBLANES)
        ls = slice(r // LANES * LANES, r // LANES * LANES + LANES)
        q8, k8, c8 = qs[rs], k[rs], cum[rs]
        tile = attn_s[rs, ls]
        for m in range(SUBLANES):
            w = jnp.exp2(c8 - c8[m:m + 1, :])
            a = jnp.sum(q8 * w * k8[m:m + 1, :], axis=-1, keepdims=True)
            a = jnp.where(rows >= m, a, 0.0)
            tile = jnp.where(lane == r % LANES + m, a, tile)
        attn_s[rs, ls] = tile
    o = o + jnp.dot(attn_s[...].astype(BF16), vb, preferred_element_type=F32)

    last = cum[c - 1:c, :]
    kd = (k * jnp.exp2(last - cum)).astype(BF16)
    state_t[...] = st * jnp.exp2(last) + lax.dot_general(vb, kd, TN_DIMS, preferred_element_type=F32)

    ms = jnp.mean(o * o, axis=-1, keepdims=True)
    y = o * lax.rsqrt(ms + EPS) * gain_ref[...]
    o_ref[...] = (_silu(g_ref[...]) * y).astype(o_ref.dtype)


def gla_heads(proj, ga, w_up, b, gain):
    s = proj.shape[0]
    c = GLA_CHUNK
    qk_base = 4 * RET_WIDTH // GLA_DK
    v_base = (4 * RET_WIDTH + 2 * GLA_KEY_WIDTH) // GLA_DV
    w_up_pad = jnp.zeros((GATE_PAD, GLA_KEY_WIDTH), F32).at[:GLA_GATE_RANK].set(w_up)
    return pl.pallas_call(
        _gla_kernel,
        grid=(GLA_HEADS, s // c),
        in_specs=[pl.BlockSpec((c, GLA_DK), lambda h, n: (n, qk_base + h)),
                  pl.BlockSpec((c, GLA_DK), lambda h, n: (n, qk_base + GLA_HEADS + h)),
                  pl.BlockSpec((c, GLA_DV), lambda h, n: (n, v_base + h)),
                  pl.BlockSpec((c, GLA_DV), lambda h, n: (n, v_base + GLA_HEADS + h)),
                  pl.BlockSpec((c, GATE_PAD), lambda h, n: (n, 0)),
                  pl.BlockSpec((GATE_PAD, GLA_DK), lambda h, n: (0, h)),
                  pl.BlockSpec((1, GLA_DK), lambda h, n: (0, h)),
                  pl.BlockSpec((1, GLA_DV), lambda h, n: (0, h))],
        out_specs=pl.BlockSpec((c, GLA_DV), lambda h, n: (n, h)),
        out_shape=jax.ShapeDtypeStruct((s, GLA_WIDTH), BF16),
        scratch_shapes=[pltpu.VMEM((GLA_DV, GLA_DK), F32),
                        pltpu.VMEM((c, c), jnp.int32),
                        pltpu.VMEM((c, c), BF16),
                        pltpu.VMEM((c, c), F32)],
        compiler_params=_params("arbitrary", "arbitrary"),
        name="gla",
    )(proj, proj, proj, proj, ga, w_up_pad, b.reshape(1, GLA_KEY_WIDTH), gain.reshape(1, GLA_WIDTH))


def kernel(x, positions, mix_norm, w_in, gla_w_up, gla_b, ret_gain, gla_gain, w_out, ffn_norm,
           w_gate, w_up, w_down, final_norm):
    b, s, d = x.shape
    x = x.reshape(b * s, d)
    cos, sin = rope_tables(positions.reshape(b * s))
    w_in_t = jnp.swapaxes(w_in, 1, 2)
    for l in range(DEPTH):
        h = rmsnorm(x, mix_norm[l], BF16)
        proj = matmul([(h, w_in_t, l, 0, 0)], MAIN_WIDTH, w_transposed=True, name="in_proj")
        ga = matmul([(h, w_in_t, l, 0, MAIN_WIDTH // GATE_PAD)], GATE_PAD, tn=GATE_PAD,
                    valid_cols=GLA_GATE_RANK, w_transposed=True, name="gate_proj")
        r_out = retention_heads(proj, cos, sin, ret_gain[l])
        g_out = gla_heads(proj, ga, gla_w_up[l], gla_b[l], gla_gain[l])
        x = matmul([(r_out, w_out, l, 0, 0), (g_out, w_out, l, 1, 0)], d, resid=x,
                   tm=1024, tn=512, name="out_proj")
        h = rmsnorm(x, ffn_norm[l], BF16)
        hid = ffn_up(h, w_gate, w_up, l)
        wd = cast_rows_bf16(w_down, l)
        x = matmul([(hid, wd, None, 0, 0)], d, resid=x, tm=512, tn=512, name="ffn_down")
    return rmsnorm(x, final_norm, F32).reshape(b, s, d)
```

```python
import functools

import numpy as np
import jax
import jax.numpy as jnp
from jax import lax
from jax.experimental import pallas as pl
from jax.experimental.pallas import tpu as pltpu

F32 = jnp.float32
BF16 = jnp.bfloat16

D_MODEL = 4096
DEPTH = 2
RET_HEADS = 8
RET_D = 256
RET_WIDTH = RET_HEADS * RET_D
GLA_HEADS = 4
GLA_DK = 256
GLA_DV = 512
GLA_KEY_WIDTH = GLA_HEADS * GLA_DK
GLA_WIDTH = GLA_HEADS * GLA_DV
GLA_GATE_RANK = 16
GLA_GATE_TAU = 16.0
FFN_HIDDEN = 11008
ROPE_BASE = 10000.0
EPS = 1e-6

MAIN_WIDTH = 4 * RET_WIDTH + 2 * GLA_KEY_WIDTH + 2 * GLA_WIDTH
LANES = 128
SUBLANES = 8
GATE_PAD = LANES

RET_CHUNK = 256
RET_CHUNKS_PER_STEP = 4
GLA_CHUNK = 256
GLA_LEVELS = (7, 6, 5, 4, 3)

VMEM_LIMIT = 56 * 1024 * 1024

NT_DIMS = (((1,), (1,)), ((), ()))
TN_DIMS = (((0,), (0,)), ((), ()))


def _params(*sem):
    return pltpu.CompilerParams(dimension_semantics=sem, vmem_limit_bytes=VMEM_LIMIT)


def _silu(x):
    return x * (1.0 / (1.0 + jnp.exp(-x)))


def _rmsnorm_kernel(x_ref, g_ref, o_ref):
    x = x_ref[...]
    ms = jnp.mean(x * x, axis=-1, keepdims=True)
    o_ref[...] = (x * lax.rsqrt(ms + EPS) * g_ref[...]).astype(o_ref.dtype)


def rmsnorm(x, gain, out_dtype, tm=256):
    s, d = x.shape
    return pl.pallas_call(
        _rmsnorm_kernel,
        grid=(s // tm,),
        in_specs=[pl.BlockSpec((tm, d), lambda i: (i, 0)),
                  pl.BlockSpec((1, d), lambda i: (0, 0))],
        out_specs=pl.BlockSpec((tm, d), lambda i: (i, 0)),
        out_shape=jax.ShapeDtypeStruct((s, d), out_dtype),
        compiler_params=_params("arbitrary"),
        name="rmsnorm",
    )(x, gain.reshape(1, d))


def _mm_kernel(*refs, n_pairs, has_resid, valid_cols, w_transposed):
    o_ref = refs[-1]
    acc = None
    for p in range(n_pairs):
        w = refs[2 * p + 1][...]
        if valid_cols is not None:
            col = lax.broadcasted_iota(jnp.int32, w.shape, 0 if w_transposed else 1)
            w = jnp.where(col < valid_cols, w, 0.0)
        if w_transposed:
            part = lax.dot_general(refs[2 * p][...], w.astype(BF16), NT_DIMS, preferred_element_type=F32)
        else:
            part = jnp.dot(refs[2 * p][...], w.astype(BF16), preferred_element_type=F32)
        acc = part if acc is None else acc + part
    if has_resid:
        acc = acc + refs[2 * n_pairs][...]
    o_ref[...] = acc.astype(o_ref.dtype)


def matmul(pairs, n, resid=None, tm=2048, tn=256, valid_cols=None, w_transposed=False,
           out_dtype=F32, single_buffer_a=False, name="matmul"):
    m = pairs[0][0].shape[0]
    in_specs, args = [], []
    a_mode = pl.Buffered(1) if single_buffer_a else None
    for a, w, layer, rb, cb0 in pairs:
        k = a.shape[1]
        in_specs.append(pl.BlockSpec((tm, k), lambda i, j: (i, 0), pipeline_mode=a_mode))
        if w_transposed:
            in_specs.append(pl.BlockSpec((None, tn, k), lambda i, j, l=layer, r=rb, c=cb0: (l, c + j, r)))
        elif w.ndim == 3:
            in_specs.append(pl.BlockSpec((None, k, tn), lambda i, j, l=layer, r=rb, c=cb0: (l, r, c + j)))
        else:
            in_specs.append(pl.BlockSpec((k, tn), lambda i, j, r=rb, c=cb0: (r, c + j)))
        args += [a, w]
    if resid is not None:
        in_specs.append(pl.BlockSpec((tm, tn), lambda i, j: (i, j)))
        args.append(resid)
    return pl.pallas_call(
        functools.partial(_mm_kernel, n_pairs=len(pairs), has_resid=resid is not None,
                          valid_cols=valid_cols, w_transposed=w_transposed),
        grid=(m // tm, n // tn),
        in_specs=in_specs,
        out_specs=pl.BlockSpec((tm, tn), lambda i, j: (i, j)),
        out_shape=jax.ShapeDtypeStruct((m, n), out_dtype),
        compiler_params=_params("arbitrary", "arbitrary"),
        name=name,
    )(*args)


def _ffn_up_kernel(h_ref, wg_ref, wu_ref, o_ref):
    h = h_ref[...]
    g = jnp.dot(h, wg_ref[...].astype(BF16), preferred_element_type=F32)
    u = jnp.dot(h, wu_ref[...].astype(BF16), preferred_element_type=F32)
    o_ref[...] = (_silu(g) * u).astype(o_ref.dtype)


def ffn_up(h, w_gate, w_up, layer, tm=2048, tn=256):
    m, k = h.shape
    n = w_gate.shape[2]
    w_spec = pl.BlockSpec((None, k, tn), lambda i, j: (layer, 0, j))
    return pl.pallas_call(
        _ffn_up_kernel,
        grid=(m // tm, n // tn),
        in_specs=[pl.BlockSpec((tm, k), lambda i, j: (i, 0)), w_spec, w_spec],
        out_specs=pl.BlockSpec((tm, tn), lambda i, j: (i, j)),
        out_shape=jax.ShapeDtypeStruct((m, n), BF16),
        compiler_params=_params("arbitrary", "arbitrary"),
        name="ffn_up",
    )(h, w_gate, w_up)


def _cast_kernel(x_ref, o_ref):
    o_ref[...] = x_ref[...].astype(o_ref.dtype)


def cast_rows_bf16(w, layer, tr=688):
    _, r, c = w.shape
    return pl.pallas_call(
        _cast_kernel,
        grid=(r // tr,),
        in_specs=[pl.BlockSpec((None, tr, c), lambda i: (layer, i, 0))],
        out_specs=pl.BlockSpec((tr, c), lambda i: (i, 0)),
        out_shape=jax.ShapeDtypeStruct((r, c), BF16),
        compiler_params=_params("arbitrary"),
        name="cast_w_down",
    )(w)


def _rope_kernel(pos_ref, inv_ref, cos_ref, sin_ref):
    ang = pos_ref[...].astype(F32) * inv_ref[...]
    cos_ref[...] = jnp.cos(ang)
    sin_ref[...] = jnp.sin(ang)


def rope_tables(positions, tm=1024):
    s = positions.shape[0]
    tm = min(tm, s)
    half = RET_D // 2
    inv_freq = (ROPE_BASE ** (-np.arange(half, dtype=np.float32) / half)).astype(np.float32)
    return pl.pallas_call(
        _rope_kernel,
        grid=(s // tm,),
        in_specs=[pl.BlockSpec((tm, 1), lambda i: (i, 0)),
                  pl.BlockSpec((1, half), lambda i: (0, 0))],
        out_specs=[pl.BlockSpec((tm, half), lambda i: (i, 0))] * 2,
        out_shape=[jax.ShapeDtypeStruct((s, half), F32)] * 2,
        compiler_params=_params("arbitrary"),
        name="rope_tables",
    )(positions.reshape(s, 1), jnp.asarray(inv_freq).reshape(1, half))


def _ret_kernel(lg_ref, q_ref, k_ref, v_ref, g_ref, cos_ref, sin_ref, gain_ref, o_ref,
                state, decay, qdec, kdec):
    c = RET_CHUNK
    lg = lg_ref[pl.program_id(0)]

    @pl.when(pl.program_id(1) == 0)
    def _():
        state[...] = jnp.zeros_like(state)
        i = lax.broadcasted_iota(jnp.int32, (c, c), 0)
        j = lax.broadcasted_iota(jnp.int32, (c, c), 1)
        diff = (i - j).astype(F32)
        decay[...] = jnp.where(diff >= 0, jnp.exp(lg * jnp.maximum(diff, 0.0)), 0.0)
        r = lax.broadcasted_iota(jnp.int32, (c, RET_D), 0).astype(F32)
        qdec[...] = jnp.exp(lg * (r + 1.0))
        kdec[...] = jnp.exp(lg * (c - 1.0 - r))

    half = RET_D // 2
    cdec = jnp.exp(jnp.zeros((1, RET_D), F32) + lg * c)
    st = state[...]
    for ci in range(RET_CHUNKS_PER_STEP):
        rs = slice(ci * c, (ci + 1) * c)
        cos = cos_ref[rs, :]
        sin = sin_ref[rs, :]

        def rot(t):
            t1, t2 = t[:, :half], t[:, half:]
            return jnp.concatenate([t1 * cos - t2 * sin, t1 * sin + t2 * cos], axis=1)

        q = rot(q_ref[rs, :].astype(F32))
        k = rot(k_ref[rs, :].astype(F32)) * (RET_D ** -0.5)
        vb = v_ref[rs, :].astype(BF16)
        qb = q.astype(BF16)
        scores = lax.dot_general(qb, k.astype(BF16), NT_DIMS, preferred_element_type=F32) * decay[...]
        o = (jnp.dot(scores.astype(BF16), vb, preferred_element_type=F32)
             + jnp.dot(qb, st.astype(BF16), preferred_element_type=F32) * qdec[...])
        kd = (k * kdec[...]).astype(BF16)
        st = st * cdec + lax.dot_general(kd, vb, TN_DIMS, preferred_element_type=F32)

        mu = jnp.mean(o, axis=-1, keepdims=True)
        oc = o - mu
        var = jnp.mean(oc * oc, axis=-1, keepdims=True)
        y = oc * lax.rsqrt(var + EPS) * gain_ref[...]
        o_ref[rs, :] = (_silu(g_ref[rs, :].astype(F32)) * y).astype(o_ref.dtype)
    state[...] = st


def retention_heads(proj, cos, sin, gain):
    s = proj.shape[0]
    c = RET_CHUNK
    log_gamma = np.log1p(-np.exp2(-5.0 - np.arange(RET_HEADS, dtype=np.float32))).astype(np.float32)
    col = lambda base: (lambda h, n, lg: (n, base + h))
    rows = min(c * RET_CHUNKS_PER_STEP, s)
    grid_spec = pltpu.PrefetchScalarGridSpec(
        num_scalar_prefetch=1,
        grid=(RET_HEADS, s // rows),
        in_specs=[pl.BlockSpec((rows, RET_D), col(0)),
                  pl.BlockSpec((rows, RET_D), col(RET_HEADS)),
                  pl.BlockSpec((rows, RET_D), col(2 * RET_HEADS)),
                  pl.BlockSpec((rows, RET_D), col(3 * RET_HEADS)),
                  pl.BlockSpec((rows, RET_D // 2), lambda h, n, lg: (n, 0)),
                  pl.BlockSpec((rows, RET_D // 2), lambda h, n, lg: (n, 0)),
                  pl.BlockSpec((1, RET_D), lambda h, n, lg: (0, h))],
        out_specs=pl.BlockSpec((rows, RET_D), lambda h, n, lg: (n, h)),
        scratch_shapes=[pltpu.VMEM((RET_D, RET_D), F32),
                        pltpu.VMEM((c, c), F32),
                        pltpu.VMEM((c, RET_D), F32),
                        pltpu.VMEM((c, RET_D), F32)],
    )
    return pl.pallas_call(
        _ret_kernel,
        grid_spec=grid_spec,
        out_shape=jax.ShapeDtypeStruct((s, RET_WIDTH + GLA_WIDTH), BF16),
        compiler_params=_params("arbitrary", "arbitrary"),
        name="retention",
    )(jnp.asarray(log_gamma), proj, proj, proj, proj, cos, sin, gain.reshape(1, RET_WIDTH))


def _split3(x):
    hi = x.astype(BF16)
    r1 = x - hi.astype(F32)
    mid = r1.astype(BF16)
    lo = (r1 - mid.astype(F32)).astype(BF16)
    return hi, mid, lo


def _gla_kernel(q_ref, k_ref, v_ref, g_ref, a_ref, wup_ref, b_ref, gain_ref, mixed_ref, o_ref,
                state_t, pair_level, tri, attn_s):
    del mixed_ref
    c = GLA_CHUNK
    log2e = 1.4426950408889634

    @pl.when(pl.program_id(1) == 0)
    def _():
        state_t[...] = jnp.zeros_like(state_t)
        i = lax.broadcasted_iota(jnp.int32, (c, c), 0)
        j = lax.broadcasted_iota(jnp.int32, (c, c), 1)
        tri[...] = jnp.where(i >= j, 1.0, 0.0).astype(BF16)
        x = jnp.where(i > j, jnp.bitwise_xor(i, j), 0)
        lvl = jnp.full((c, c), GLA_LEVELS[-1] - 1, jnp.int32)
        for p in GLA_LEVELS:
            lvl = lvl + jnp.where(x >= (1 << p), 1, 0)
        pair_level[...] = lvl

    a_hi, a_mid, a_lo = _split3(a_ref[...])
    w_hi, w_mid, w_lo = _split3(wup_ref[...])
    dotf = lambda x, y: jnp.dot(x, y, preferred_element_type=F32)
    z = (dotf(a_hi, w_hi) + (dotf(a_hi, w_mid) + dotf(a_mid, w_hi))
         + (dotf(a_hi, w_lo) + dotf(a_lo, w_hi) + dotf(a_mid, w_mid))) + b_ref[...]
    la = (jnp.minimum(z, 0.0) - jnp.log1p(jnp.exp(-jnp.abs(z)))) * (log2e / GLA_GATE_TAU)
    l_hi, l_mid, l_lo = _split3(la)
    t = tri[...]
    cum = dotf(t, l_hi) + dotf(t, l_mid) + dotf(t, l_lo)

    qs = q_ref[...].astype(F32) * (GLA_DK ** -0.5)
    k = k_ref[...].astype(F32)
    vb = v_ref[...].astype(BF16)
    st = state_t[...]

    o = lax.dot_general((qs * jnp.exp2(cum)).astype(BF16), st.astype(BF16), NT_DIMS,
                        preferred_element_type=F32)

    attn_s[...] = jnp.zeros_like(attn_s)
    for p in GLA_LEVELS:
        half = 1 << p
        nb = c // (2 * half)
        q_rows, k_rows = [], []
        for blk in range(nb):
            up = slice(blk * 2 * half, blk * 2 * half + half)
            lo = slice(blk * 2 * half + half, (blk + 1) * 2 * half)
            bnd = cum[up.stop - 1:up.stop, :]
            e_lo = jnp.exp2(cum[lo] - bnd)
            k_rows += [k[up] * jnp.exp2(bnd - cum[up]), k[lo] * e_lo]
            q_rows.append(qs[lo] * e_lo)
        q_t = q_rows[0] if nb == 1 else jnp.concatenate(q_rows, axis=0)
        k_t = jnp.concatenate(k_rows, axis=0)
        part = lax.dot_general(q_t.astype(BF16), k_t.astype(BF16), NT_DIMS,
                               preferred_element_type=F32)
        for blk in range(nb):
            lo = slice(blk * 2 * half + half, (blk + 1) * 2 * half)
            attn_s[lo, :] = jnp.where(pair_level[lo, :] == p,
                                      part[blk * half:(blk + 1) * half, :], attn_s[lo, :])

    rows = lax.broadcasted_iota(jnp.int32, (SUBLANES, 1), 0)
    lane = lax.broadcasted_iota(jnp.int32, (SUBLANES, LANES), 1)
    for blk in range(c // SUBLANES):
        r = blk * SUBLANES
        rs = slice(r, r + SUBLANES)
        ls = slice(r // LANES * LANES, r // LANES * LANES + LANES)
        q8, k8, c8 = qs[rs], k[rs], cum[rs]
        tile = attn_s[rs, ls]
        for m in range(SUBLANES):
            w = jnp.exp2(c8 - c8[m:m + 1, :])
            a = jnp.sum(q8 * w * k8[m:m + 1, :], axis=-1, keepdims=True)
            a = jnp.where(rows >= m, a, 0.0)
            tile = jnp.where(lane == r % LANES + m, a, tile)
        attn_s[rs, ls] = tile
    o = o + jnp.dot(attn_s[...].astype(BF16), vb, preferred_element_type=F32)

    last = cum[c - 1:c, :]
    kd = (k * jnp.exp2(last - cum)).astype(BF16)
    state_t[...] = st * jnp.exp2(last) + lax.dot_general(vb, kd, TN_DIMS, preferred_element_type=F32)

    ms = jnp.mean(o * o, axis=-1, keepdims=True)
    y = o * lax.rsqrt(ms + EPS) * gain_ref[...]
    o_ref[...] = (_silu(g_ref[...].astype(F32)) * y).astype(o_ref.dtype)


def gla_heads(proj, ga, w_up, b, gain, mixed):
    s = proj.shape[0]
    c = GLA_CHUNK
    qk_base = 4 * RET_WIDTH // GLA_DK
    v_base = (4 * RET_WIDTH + 2 * GLA_KEY_WIDTH) // GLA_DV
    w_up_pad = jnp.zeros((GATE_PAD, GLA_KEY_WIDTH), F32).at[:GLA_GATE_RANK].set(w_up)
    return pl.pallas_call(
        _gla_kernel,
        grid=(GLA_HEADS, s // c),
        in_specs=[pl.BlockSpec((c, GLA_DK), lambda h, n: (n, qk_base + h)),
                  pl.BlockSpec((c, GLA_DK), lambda h, n: (n, qk_base + GLA_HEADS + h)),
                  pl.BlockSpec((c, GLA_DV), lambda h, n: (n, v_base + h)),
                  pl.BlockSpec((c, GLA_DV), lambda h, n: (n, v_base + GLA_HEADS + h)),
                  pl.BlockSpec((c, GATE_PAD), lambda h, n: (n, 0)),
                  pl.BlockSpec((GATE_PAD, GLA_DK), lambda h, n: (0, h)),
                  pl.BlockSpec((1, GLA_DK), lambda h, n: (0, h)),
                  pl.BlockSpec((1, GLA_DV), lambda h, n: (0, h)),
                  pl.BlockSpec(memory_space=pl.ANY)],
        out_specs=pl.BlockSpec((c, GLA_DV), lambda h, n: (n, RET_WIDTH // GLA_DV + h)),
        out_shape=jax.ShapeDtypeStruct(mixed.shape, mixed.dtype),
        input_output_aliases={8: 0},
        scratch_shapes=[pltpu.VMEM((GLA_DV, GLA_DK), F32),
                        pltpu.VMEM((c, c), jnp.int32),
                        pltpu.VMEM((c, c), BF16),
                        pltpu.VMEM((c, c), F32)],
        compiler_params=_params("arbitrary", "arbitrary"),
        name="gla",
    )(proj, proj, proj, proj, ga, w_up_pad, b.reshape(1, GLA_KEY_WIDTH), gain.reshape(1, GLA_WIDTH),
      mixed)


def kernel(x, positions, mix_norm, w_in, gla_w_up, gla_b, ret_gain, gla_gain, w_out, ffn_norm,
           w_gate, w_up, w_down, final_norm):
    b, s, d = x.shape
    x = x.reshape(b * s, d)
    cos, sin = rope_tables(positions.reshape(b * s))
    w_in_t = jnp.swapaxes(w_in, 1, 2)
    for l in range(DEPTH):
        h = rmsnorm(x, mix_norm[l], BF16)
        proj = matmul([(h, w_in_t, l, 0, 0)], MAIN_WIDTH, tn=512, w_transposed=True,
                      out_dtype=BF16, single_buffer_a=True, name="in_proj")
        ga = matmul([(h, w_in_t, l, 0, MAIN_WIDTH // GATE_PAD)], GATE_PAD, tn=GATE_PAD,
                    valid_cols=GLA_GATE_RANK, w_transposed=True, name="gate_proj")
        mixed = retention_heads(proj, cos, sin, ret_gain[l])
        mixed = gla_heads(proj, ga, gla_w_up[l], gla_b[l], gla_gain[l], mixed)
        x = matmul([(mixed, w_out, l, 0, 0)], d, resid=x, tm=1024, tn=512, name="out_proj")
        h = rmsnorm(x, ffn_norm[l], BF16)
        hid = ffn_up(h, w_gate, w_up, l)
        wd = cast_rows_bf16(w_down, l)
        x = matmul([(hid, wd, None, 0, 0)], d, resid=x, tm=512, tn=512, name="ffn_down")
    return rmsnorm(x, final_norm, F32).reshape(b, s, d)
```

```python
import numpy as np
import jax
import jax.numpy as jnp
from jax import lax
from jax.experimental import pallas as pl
from jax.experimental.pallas import tpu as pltpu

F32 = jnp.float32
BF16 = jnp.bfloat16

D_MODEL = 4096
DEPTH = 2
RET_HEADS = 8
RET_D = 256
RET_WIDTH = RET_HEADS * RET_D
GLA_HEADS = 4
GLA_DK = 256
GLA_DV = 512
GLA_KEY_WIDTH = GLA_HEADS * GLA_DK
GLA_WIDTH = GLA_HEADS * GLA_DV
GLA_GATE_RANK = 16
GLA_GATE_TAU = 16.0
FFN_HIDDEN = 11008
ROPE_BASE = 10000.0
EPS = 1e-6

MAIN_WIDTH = 4 * RET_WIDTH + 2 * GLA_KEY_WIDTH + 2 * GLA_WIDTH
LANES = 128
SUBLANES = 8
GATE_PAD = LANES

RET_CHUNK = 256
RET_CHUNKS_PER_STEP = 4
GLA_CHUNK = 256
GLA_LEVELS = (7, 6, 5, 4, 3)

VMEM_LIMIT = 56 * 1024 * 1024

NT_DIMS = (((1,), (1,)), ((), ()))
TN_DIMS = (((0,), (0,)), ((), ()))


def _params(*sem):
    return pltpu.CompilerParams(dimension_semantics=sem, vmem_limit_bytes=VMEM_LIMIT)


def _silu(x):
    return x * (1.0 / (1.0 + jnp.exp(-x)))


def _rmsnorm_kernel(x_ref, g_ref, o_ref):
    x = x_ref[...]
    ms = jnp.mean(x * x, axis=-1, keepdims=True)
    o_ref[...] = (x * lax.rsqrt(ms + EPS) * g_ref[...]).astype(o_ref.dtype)


def rmsnorm(x, gain, out_dtype, tm=256):
    s, d = x.shape
    return pl.pallas_call(
        _rmsnorm_kernel,
        grid=(s // tm,),
        in_specs=[pl.BlockSpec((tm, d), lambda i: (i, 0)),
                  pl.BlockSpec((1, d), lambda i: (0, 0))],
        out_specs=pl.BlockSpec((tm, d), lambda i: (i, 0)),
        out_shape=jax.ShapeDtypeStruct((s, d), out_dtype),
        compiler_params=_params("arbitrary"),
        name="rmsnorm",
    )(x, gain.reshape(1, d))


def _row_block_spec(tm, k, single_buffer=True):
    return pl.BlockSpec((tm, k), lambda i, j: (i, 0),
                        pipeline_mode=pl.Buffered(1) if single_buffer else None)


def _mm_resid_kernel(a_ref, w_ref, r_ref, o_ref):
    acc = jnp.dot(a_ref[...], w_ref[...].astype(BF16), preferred_element_type=F32)
    o_ref[...] = acc + r_ref[...]


def matmul_resid(a, w, resid, layer=None, tm=2048, tn=512, single_buffer_a=True, name="matmul"):
    m, k = a.shape
    n = w.shape[-1]
    if w.ndim == 3:
        w_spec = pl.BlockSpec((None, k, tn), lambda i, j: (layer, 0, j))
    else:
        w_spec = pl.BlockSpec((k, tn), lambda i, j: (0, j))
    return pl.pallas_call(
        _mm_resid_kernel,
        grid=(m // tm, n // tn),
        in_specs=[_row_block_spec(tm, k, single_buffer_a), w_spec,
                  pl.BlockSpec((tm, tn), lambda i, j: (i, j))],
        out_specs=pl.BlockSpec((tm, tn), lambda i, j: (i, j)),
        out_shape=jax.ShapeDtypeStruct((m, n), F32),
        compiler_params=_params("arbitrary", "arbitrary"),
        name=name,
    )(a, w, resid)


def _in_proj_kernel(h_ref, w_ref, wlr_ref, o_ref, ga_ref):
    o_ref[...] = lax.dot_general(h_ref[...], w_ref[...].astype(BF16), NT_DIMS,
                                 preferred_element_type=F32).astype(o_ref.dtype)

    @pl.when(pl.program_id(1) == 0)
    def _():
        wlr = wlr_ref[...]
        row = lax.broadcasted_iota(jnp.int32, wlr.shape, 0)
        wlr = jnp.where(row < GLA_GATE_RANK, wlr, 0.0)
        ga_ref[...] = lax.dot_general(h_ref[...], wlr.astype(BF16), NT_DIMS,
                                      preferred_element_type=F32)


def in_proj(h, w_in_t, layer, tm=2048, tn=512):
    m, k = h.shape
    return pl.pallas_call(
        _in_proj_kernel,
        grid=(m // tm, MAIN_WIDTH // tn),
        in_specs=[_row_block_spec(tm, k),
                  pl.BlockSpec((None, tn, k), lambda i, j: (layer, j, 0)),
                  pl.BlockSpec((None, GATE_PAD, k), lambda i, j: (layer, MAIN_WIDTH // GATE_PAD, 0))],
        out_specs=[pl.BlockSpec((tm, tn), lambda i, j: (i, j)),
                   pl.BlockSpec((tm, GATE_PAD), lambda i, j: (i, 0))],
        out_shape=[jax.ShapeDtypeStruct((m, MAIN_WIDTH), BF16),
                   jax.ShapeDtypeStruct((m, GATE_PAD), F32)],
        compiler_params=_params("arbitrary", "arbitrary"),
        name="in_proj",
    )(h, w_in_t, w_in_t)


def _ffn_up_kernel(h_ref, wg_ref, wu_ref, wd_ref, o_ref, wd_bf16_ref):
    h = h_ref[...]
    g = jnp.dot(h, wg_ref[...].astype(BF16), preferred_element_type=F32)
    u = jnp.dot(h, wu_ref[...].astype(BF16), preferred_element_type=F32)
    o_ref[...] = (_silu(g) * u).astype(o_ref.dtype)

    @pl.when(pl.program_id(0) == 0)
    def _():
        wd_bf16_ref[...] = wd_ref[...].astype(BF16)


def ffn_up(h, w_gate, w_up, w_down, layer, tm=2048, tn=256):
    m, k = h.shape
    n = w_gate.shape[2]
    d = w_down.shape[2]
    nj = n // tn
    w_spec = pl.BlockSpec((None, k, tn), lambda i, j: (layer, 0, j))
    first_pass_block = lambda i, j: jnp.where(i == 0, j, nj - 1)
    return pl.pallas_call(
        _ffn_up_kernel,
        grid=(m // tm, nj),
        in_specs=[_row_block_spec(tm, k), w_spec, w_spec,
                  pl.BlockSpec((None, tn, d), lambda i, j: (layer, first_pass_block(i, j), 0))],
        out_specs=[pl.BlockSpec((tm, tn), lambda i, j: (i, j)),
                   pl.BlockSpec((tn, d), lambda i, j: (first_pass_block(i, j), 0))],
        out_shape=[jax.ShapeDtypeStruct((m, n), BF16),
                   jax.ShapeDtypeStruct((n, d), BF16)],
        compiler_params=_params("arbitrary", "arbitrary"),
        name="ffn_up",
    )(h, w_gate, w_up, w_down)


def _rope_kernel(pos_ref, inv_ref, cos_ref, sin_ref):
    ang = pos_ref[...].astype(F32) * inv_ref[...]
    cos_ref[...] = jnp.cos(ang)
    sin_ref[...] = jnp.sin(ang)


def rope_tables(positions, tm=1024):
    s = positions.shape[0]
    tm = min(tm, s)
    half = RET_D // 2
    inv_freq = (ROPE_BASE ** (-np.arange(half, dtype=np.float32) / half)).astype(np.float32)
    return pl.pallas_call(
        _rope_kernel,
        grid=(s // tm,),
        in_specs=[pl.BlockSpec((tm, 1), lambda i: (i, 0)),
                  pl.BlockSpec((1, half), lambda i: (0, 0))],
        out_specs=[pl.BlockSpec((tm, half), lambda i: (i, 0))] * 2,
        out_shape=[jax.ShapeDtypeStruct((s, half), F32)] * 2,
        compiler_params=_params("arbitrary"),
        name="rope_tables",
    )(positions.reshape(s, 1), jnp.asarray(inv_freq).reshape(1, half))


def _ret_kernel(lg_ref, q_ref, k_ref, v_ref, g_ref, cos_ref, sin_ref, gain_ref, o_ref,
                state, decay, qdec, kdec):
    c = RET_CHUNK
    lg = lg_ref[pl.program_id(0)]

    @pl.when(pl.program_id(1) == 0)
    def _():
        state[...] = jnp.zeros_like(state)
        i = lax.broadcasted_iota(jnp.int32, (c, c), 0)
        j = lax.broadcasted_iota(jnp.int32, (c, c), 1)
        diff = (i - j).astype(F32)
        decay[...] = jnp.where(diff >= 0, jnp.exp(lg * jnp.maximum(diff, 0.0)), 0.0)
        r = lax.broadcasted_iota(jnp.int32, (c, RET_D), 0).astype(F32)
        qdec[...] = jnp.exp(lg * (r + 1.0))
        kdec[...] = jnp.exp(lg * (c - 1.0 - r))

    half = RET_D // 2
    cdec = jnp.exp(jnp.zeros((1, RET_D), F32) + lg * c)
    st = state[...]
    for ci in range(RET_CHUNKS_PER_STEP):
        rs = slice(ci * c, (ci + 1) * c)
        cos = cos_ref[rs, :]
        sin = sin_ref[rs, :]

        def rot(t):
            t1, t2 = t[:, :half], t[:, half:]
            return jnp.concatenate([t1 * cos - t2 * sin, t1 * sin + t2 * cos], axis=1)

        q = rot(q_ref[rs, :].astype(F32))
        k = rot(k_ref[rs, :].astype(F32)) * (RET_D ** -0.5)
        vb = v_ref[rs, :].astype(BF16)
        qb = q.astype(BF16)
        scores = lax.dot_general(qb, k.astype(BF16), NT_DIMS, preferred_element_type=F32) * decay[...]
        o = (jnp.dot(scores.astype(BF16), vb, preferred_element_type=F32)
             + jnp.dot(qb, st.astype(BF16), preferred_element_type=F32) * qdec[...])
        kd = (k * kdec[...]).astype(BF16)
        st = st * cdec + lax.dot_general(kd, vb, TN_DIMS, preferred_element_type=F32)

        mu = jnp.mean(o, axis=-1, keepdims=True)
        oc = o - mu
        var = jnp.mean(oc * oc, axis=-1, keepdims=True)
        y = oc * lax.rsqrt(var + EPS) * gain_ref[...]
        o_ref[rs, :] = (_silu(g_ref[rs, :].astype(F32)) * y).astype(o_ref.dtype)
    state[...] = st


def retention_heads(proj, cos, sin, gain):
    s = proj.shape[0]
    c = RET_CHUNK
    log_gamma = np.log1p(-np.exp2(-5.0 - np.arange(RET_HEADS, dtype=np.float32))).astype(np.float32)
    col = lambda base: (lambda h, n, lg: (n, base + h))
    rows = min(c * RET_CHUNKS_PER_STEP, s)
    grid_spec = pltpu.PrefetchScalarGridSpec(
        num_scalar_prefetch=1,
        grid=(RET_HEADS, s // rows),
        in_specs=[pl.BlockSpec((rows, RET_D), col(0)),
                  pl.BlockSpec((rows, RET_D), col(RET_HEADS)),
                  pl.BlockSpec((rows, RET_D), col(2 * RET_HEADS)),
                  pl.BlockSpec((rows, RET_D), col(3 * RET_HEADS)),
                  pl.BlockSpec((rows, RET_D // 2), lambda h, n, lg: (n, 0)),
                  pl.BlockSpec((rows, RET_D // 2), lambda h, n, lg: (n, 0)),
                  pl.BlockSpec((1, RET_D), lambda h, n, lg: (0, h))],
        out_specs=pl.BlockSpec((rows, RET_D), lambda h, n, lg: (n, h)),
        scratch_shapes=[pltpu.VMEM((RET_D, RET_D), F32),
                        pltpu.VMEM((c, c), F32),
                        pltpu.VMEM((c, RET_D), F32),
                        pltpu.VMEM((c, RET_D), F32)],
    )
    return pl.pallas_call(
        _ret_kernel,
        grid_spec=grid_spec,
        out_shape=jax.ShapeDtypeStruct((s, RET_WIDTH + GLA_WIDTH), BF16),
        compiler_params=_params("arbitrary", "arbitrary"),
        name="retention",
    )(jnp.asarray(log_gamma), proj, proj, proj, proj, cos, sin, gain.reshape(1, RET_WIDTH))


def _split3(x):
    hi = x.astype(BF16)
    r1 = x - hi.astype(F32)
    mid = r1.astype(BF16)
    lo = (r1 - mid.astype(F32)).astype(BF16)
    return hi, mid, lo


def _gla_kernel(q_ref, k_ref, v_ref, g_ref, a_ref, wup_ref, b_ref, gain_ref, mixed_ref, o_ref,
                state_t, pair_level, tri, attn_s):
    del mixed_ref
    c = GLA_CHUNK
    log2e = 1.4426950408889634

    @pl.when(pl.program_id(1) == 0)
    def _():
        state_t[...] = jnp.zeros_like(state_t)
        i = lax.broadcasted_iota(jnp.int32, (c, c), 0)
        j = lax.broadcasted_iota(jnp.int32, (c, c), 1)
        tri[...] = jnp.where(i >= j, 1.0, 0.0).astype(BF16)
        x = jnp.where(i > j, jnp.bitwise_xor(i, j), 0)
        lvl = jnp.full((c, c), GLA_LEVELS[-1] - 1, jnp.int32)
        for p in GLA_LEVELS:
            lvl = lvl + jnp.where(x >= (1 << p), 1, 0)
        pair_level[...] = lvl

    a_hi, a_mid, a_lo = _split3(a_ref[...])
    w_hi, w_mid, w_lo = _split3(wup_ref[...])
    dotf = lambda x, y: jnp.dot(x, y, preferred_element_type=F32)
    z = (dotf(a_hi, w_hi) + (dotf(a_hi, w_mid) + dotf(a_mid, w_hi))
         + (dotf(a_hi, w_lo) + dotf(a_lo, w_hi) + dotf(a_mid, w_mid))) + b_ref[...]
    la = (jnp.minimum(z, 0.0) - jnp.log1p(jnp.exp(-jnp.abs(z)))) * (log2e / GLA_GATE_TAU)
    l_hi, l_mid, l_lo = _split3(la)
    t = tri[...]
    cum = dotf(t, l_hi) + dotf(t, l_mid) + dotf(t, l_lo)

    qs = q_ref[...].astype(F32) * (GLA_DK ** -0.5)
    k = k_ref[...].astype(F32)
    vb = v_ref[...].astype(BF16)
    st = state_t[...]

    o = lax.dot_general((qs * jnp.exp2(cum)).astype(BF16), st.astype(BF16), NT_DIMS,
                        preferred_element_type=F32)

    attn_s[...] = jnp.zeros_like(attn_s)
    for p in GLA_LEVELS:
        half = 1 << p
        nb = c // (2 * half)
        q_rows, k_rows = [], []
        for blk in range(nb):
            up = slice(blk * 2 * half, blk * 2 * half + half)
            lo = slice(blk * 2 * half + half, (blk + 1) * 2 * half)
            bnd = cum[up.stop - 1:up.stop, :]
            e_lo = jnp.exp2(cum[lo] - bnd)
            k_rows += [k[up] * jnp.exp2(bnd - cum[up]), k[lo] * e_lo]
            q_rows.append(qs[lo] * e_lo)
        q_t = q_rows[0] if nb == 1 else jnp.concatenate(q_rows, axis=0)
        k_t = jnp.concatenate(k_rows, axis=0)
        part = lax.dot_general(q_t.astype(BF16), k_t.astype(BF16), NT_DIMS,
                               preferred_element_type=F32)
        for blk in range(nb):
            lo = slice(blk * 2 * half + half, (blk + 1) * 2 * half)
            attn_s[lo, :] = jnp.where(pair_level[lo, :] == p,
                                      part[blk * half:(blk + 1) * half, :], attn_s[lo, :])

    rows = lax.broadcasted_iota(jnp.int32, (SUBLANES, 1), 0)
    lane = lax.broadcasted_iota(jnp.int32, (SUBLANES, LANES), 1)
    for blk in range(c // SUBLANES):
        r = blk * SUBLANES
        rs = slice(r, r + SUBLANES)
        ls = slice(r // LANES * LANES, r // LANES * LANES + LANES)
        q8, k8, c8 = qs[rs], k[rs], cum[rs]
        tile = attn_s[rs, ls]
        for m in range(SUBLANES):
            w = jnp.exp2(c8 - c8[m:m + 1, :])
            a = jnp.sum(q8 * w * k8[m:m + 1, :], axis=-1, keepdims=True)
            a = jnp.where(rows >= m, a, 0.0)
            tile = jnp.where(lane == r % LANES + m, a, tile)
        attn_s[rs, ls] = tile
    o = o + jnp.dot(attn_s[...].astype(BF16), vb, preferred_element_type=F32)

    last = cum[c - 1:c, :]
    kd = (k * jnp.exp2(last - cum)).astype(BF16)
    state_t[...] = st * jnp.exp2(last) + lax.dot_general(vb, kd, TN_DIMS, preferred_element_type=F32)

    ms = jnp.mean(o * o, axis=-1, keepdims=True)
    y = o * lax.rsqrt(ms + EPS) * gain_ref[...]
    o_ref[...] = (_silu(g_ref[...].astype(F32)) * y).astype(o_ref.dtype)


def gla_heads(proj, ga, w_up, b, gain, mixed):
    s = proj.shape[0]
    c = GLA_CHUNK
    qk_base = 4 * RET_WIDTH // GLA_DK
    v_base = (4 * RET_WIDTH + 2 * GLA_KEY_WIDTH) // GLA_DV
    w_up_pad = jnp.zeros((GATE_PAD, GLA_KEY_WIDTH), F32).at[:GLA_GATE_RANK].set(w_up)
    return pl.pallas_call(
        _gla_kernel,
        grid=(GLA_HEADS, s // c),
        in_specs=[pl.BlockSpec((c, GLA_DK), lambda h, n: (n, qk_base + h)),
                  pl.BlockSpec((c, GLA_DK), lambda h, n: (n, qk_base + GLA_HEADS + h)),
                  pl.BlockSpec((c, GLA_DV), lambda h, n: (n, v_base + h)),
                  pl.BlockSpec((c, GLA_DV), lambda h, n: (n, v_base + GLA_HEADS + h)),
                  pl.BlockSpec((c, GATE_PAD), lambda h, n: (n, 0)),
                  pl.BlockSpec((GATE_PAD, GLA_DK), lambda h, n: (0, h)),
                  pl.BlockSpec((1, GLA_DK), lambda h, n: (0, h)),
                  pl.BlockSpec((1, GLA_DV), lambda h, n: (0, h)),
                  pl.BlockSpec(memory_space=pl.ANY)],
        out_specs=pl.BlockSpec((c, GLA_DV), lambda h, n: (n, RET_WIDTH // GLA_DV + h)),
        out_shape=jax.ShapeDtypeStruct(mixed.shape, mixed.dtype),
        input_output_aliases={8: 0},
        scratch_shapes=[pltpu.VMEM((GLA_DV, GLA_DK), F32),
                        pltpu.VMEM((c, c), jnp.int32),
                        pltpu.VMEM((c, c), BF16),
                        pltpu.VMEM((c, c), F32)],
        compiler_params=_params("arbitrary", "arbitrary"),
        name="gla",
    )(proj, proj, proj, proj, ga, w_up_pad, b.reshape(1, GLA_KEY_WIDTH), gain.reshape(1, GLA_WIDTH),
      mixed)


def kernel(x, positions, mix_norm, w_in, gla_w_up, gla_b, ret_gain, gla_gain, w_out, ffn_norm,
           w_gate, w_up, w_down, final_norm):
    b, s, d = x.shape
    x = x.reshape(b * s, d)
    cos, sin = rope_tables(positions.reshape(b * s))
    w_in_t = jnp.swapaxes(w_in, 1, 2)
    for l in range(DEPTH):
        h = rmsnorm(x, mix_norm[l], BF16)
        proj, ga = in_proj(h, w_in_t, l)
        mixed = retention_heads(proj, cos, sin, ret_gain[l])
        mixed = gla_heads(proj, ga, gla_w_up[l], gla_b[l], gla_gain[l], mixed)
        x = matmul_resid(mixed, w_out, x, layer=l, name="out_proj")
        h = rmsnorm(x, ffn_norm[l], BF16)
        hid, wd = ffn_up(h, w_gate, w_up, w_down, l)
        x = matmul_resid(hid, wd, x, tm=512, tn=512, single_buffer_a=False, name="ffn_down")
    return rmsnorm(x, final_norm, F32).reshape(b, s, d)
```

```python
import functools

import numpy as np
import jax
import jax.numpy as jnp
from jax import lax
from jax.experimental import pallas as pl
from jax.experimental.pallas import tpu as pltpu

F32 = jnp.float32
BF16 = jnp.bfloat16

D_MODEL = 4096
DEPTH = 2
RET_HEADS = 8
RET_D = 256
RET_WIDTH = RET_HEADS * RET_D
GLA_HEADS = 4
GLA_DK = 256
GLA_DV = 512
GLA_KEY_WIDTH = GLA_HEADS * GLA_DK
GLA_WIDTH = GLA_HEADS * GLA_DV
GLA_GATE_RANK = 16
GLA_GATE_TAU = 16.0
FFN_HIDDEN = 11008
ROPE_BASE = 10000.0
EPS = 1e-6

MAIN_WIDTH = 4 * RET_WIDTH + 2 * GLA_KEY_WIDTH + 2 * GLA_WIDTH
LANES = 128
SUBLANES = 8
GATE_PAD = LANES

RET_CHUNK = 256
RET_CHUNKS_PER_STEP = 4
GLA_CHUNK = 256
GLA_LEVELS = (7, 6, 5, 4, 3)

VMEM_LIMIT = 62 * 1024 * 1024

NT_DIMS = (((1,), (1,)), ((), ()))
TN_DIMS = (((0,), (0,)), ((), ()))


def _params(*sem):
    return pltpu.CompilerParams(dimension_semantics=sem, vmem_limit_bytes=VMEM_LIMIT)


def _silu(x):
    return x * (1.0 / (1.0 + jnp.exp(-x)))


def _rmsnorm_kernel(x_ref, g_ref, o_ref):
    x = x_ref[...]
    ms = jnp.mean(x * x, axis=-1, keepdims=True)
    o_ref[...] = (x * lax.rsqrt(ms + EPS) * g_ref[...]).astype(o_ref.dtype)


def rmsnorm(x, gain, out_dtype, tm=256):
    s, d = x.shape
    return pl.pallas_call(
        _rmsnorm_kernel,
        grid=(s // tm,),
        in_specs=[pl.BlockSpec((tm, d), lambda i: (i, 0)),
                  pl.BlockSpec((1, d), lambda i: (0, 0))],
        out_specs=pl.BlockSpec((tm, d), lambda i: (i, 0)),
        out_shape=jax.ShapeDtypeStruct((s, d), out_dtype),
        compiler_params=_params("arbitrary"),
        name="rmsnorm",
    )(x, gain.reshape(1, d))


def _row_block_spec(tm, k, single_buffer=True):
    return pl.BlockSpec((tm, k), lambda i, j: (i, 0),
                        pipeline_mode=pl.Buffered(1) if single_buffer else None)


def _mm_resid_kernel(a_ref, w_ref, r_ref, o_ref):
    acc = jnp.dot(a_ref[...], w_ref[...].astype(BF16), preferred_element_type=F32)
    o_ref[...] = acc + r_ref[...]


def matmul_resid(a, w, resid, layer=None, tm=2048, tn=512, single_buffer_a=True, name="matmul"):
    m, k = a.shape
    n = w.shape[-1]
    if w.ndim == 3:
        w_spec = pl.BlockSpec((None, k, tn), lambda i, j: (layer, 0, j))
    else:
        w_spec = pl.BlockSpec((k, tn), lambda i, j: (0, j))
    return pl.pallas_call(
        _mm_resid_kernel,
        grid=(m // tm, n // tn),
        in_specs=[_row_block_spec(tm, k, single_buffer_a), w_spec,
                  pl.BlockSpec((tm, tn), lambda i, j: (i, j))],
        out_specs=pl.BlockSpec((tm, tn), lambda i, j: (i, j)),
        out_shape=jax.ShapeDtypeStruct((m, n), F32),
        compiler_params=_params("arbitrary", "arbitrary"),
        name=name,
    )(a, w, resid)


def _in_proj_kernel(h_ref, w_ref, wlr_ref, wd_ref, o_ref, ga_ref, wd_bf16_ref):
    o_ref[...] = lax.dot_general(h_ref[...], w_ref[...].astype(BF16), NT_DIMS,
                                 preferred_element_type=F32).astype(o_ref.dtype)
    wd_bf16_ref[...] = wd_ref[...].astype(BF16)

    @pl.when(pl.program_id(1) == 0)
    def _():
        wlr = wlr_ref[...]
        row = lax.broadcasted_iota(jnp.int32, wlr.shape, 0)
        wlr = jnp.where(row < GLA_GATE_RANK, wlr, 0.0)
        ga_ref[...] = lax.dot_general(h_ref[...], wlr.astype(BF16), NT_DIMS,
                                      preferred_element_type=F32)


def in_proj(h, w_in_t, w_down, layer, tm=2048, tn=512, tr=128):
    m, k = h.shape
    _, f, d = w_down.shape
    nj = MAIN_WIDTH // tn
    n_wd = f // tr
    assert n_wd * tr == f and n_wd <= (m // tm) * nj
    wd_block = lambda i, j: jnp.minimum(i * nj + j, n_wd - 1)
    return pl.pallas_call(
        _in_proj_kernel,
        grid=(m // tm, nj),
        in_specs=[_row_block_spec(tm, k),
                  pl.BlockSpec((None, tn, k), lambda i, j: (layer, j, 0)),
                  pl.BlockSpec((None, GATE_PAD, k), lambda i, j: (layer, MAIN_WIDTH // GATE_PAD, 0)),
                  pl.BlockSpec((None, tr, d), lambda i, j: (layer, wd_block(i, j), 0))],
        out_specs=[pl.BlockSpec((tm, tn), lambda i, j: (i, j)),
                   pl.BlockSpec((tm, GATE_PAD), lambda i, j: (i, 0)),
                   pl.BlockSpec((tr, d), lambda i, j: (wd_block(i, j), 0))],
        out_shape=[jax.ShapeDtypeStruct((m, MAIN_WIDTH), BF16),
                   jax.ShapeDtypeStruct((m, GATE_PAD), F32),
                   jax.ShapeDtypeStruct((f, d), BF16)],
        compiler_params=_params("arbitrary", "arbitrary"),
        name="in_proj",
    )(h, w_in_t, w_in_t, w_down)


def _ffn_up_kernel(h_ref, wg_ref, wu_ref, o_ref, *, row_splits):
    wg = wg_ref[...].astype(BF16)
    wu = wu_ref[...].astype(BF16)
    rows = h_ref.shape[0] // row_splits
    for r in range(row_splits):
        rs = slice(r * rows, (r + 1) * rows)
        g = jnp.dot(h_ref[rs, :], wg, preferred_element_type=F32)
        u = jnp.dot(h_ref[rs, :], wu, preferred_element_type=F32)
        o_ref[rs, :] = (_silu(g) * u).astype(o_ref.dtype)


def ffn_up(h, w_gate, w_up, layer, tm=4096, tn=256, row_splits=4):
    m, k = h.shape
    n = w_gate.shape[2]
    w_spec = pl.BlockSpec((None, k, tn), lambda i, j: (layer, 0, j))
    return pl.pallas_call(
        functools.partial(_ffn_up_kernel, row_splits=row_splits),
        grid=(m // tm, n // tn),
        in_specs=[_row_block_spec(tm, k), w_spec, w_spec],
        out_specs=pl.BlockSpec((tm, tn), lambda i, j: (i, j)),
        out_shape=jax.ShapeDtypeStruct((m, n), BF16),
        compiler_params=_params("arbitrary", "arbitrary"),
        name="ffn_up",
    )(h, w_gate, w_up)


def _rope_kernel(pos_ref, inv_ref, cos_ref, sin_ref):
    ang = pos_ref[...].astype(F32) * inv_ref[...]
    cos_ref[...] = jnp.cos(ang)
    sin_ref[...] = jnp.sin(ang)


def rope_tables(positions, tm=1024):
    s = positions.shape[0]
    tm = min(tm, s)
    half = RET_D // 2
    inv_freq = (ROPE_BASE ** (-np.arange(half, dtype=np.float32) / half)).astype(np.float32)
    return pl.pallas_call(
        _rope_kernel,
        grid=(s // tm,),
        in_specs=[pl.BlockSpec((tm, 1), lambda i: (i, 0)),
                  pl.BlockSpec((1, half), lambda i: (0, 0))],
        out_specs=[pl.BlockSpec((tm, half), lambda i: (i, 0))] * 2,
        out_shape=[jax.ShapeDtypeStruct((s, half), F32)] * 2,
        compiler_params=_params("arbitrary"),
        name="rope_tables",
    )(positions.reshape(s, 1), jnp.asarray(inv_freq).reshape(1, half))


def _ret_kernel(lg_ref, q_ref, k_ref, v_ref, g_ref, cos_ref, sin_ref, gain_ref, o_ref,
                state, decay, qdec, kdec):
    c = RET_CHUNK
    lg = lg_ref[pl.program_id(0)]

    @pl.when(pl.program_id(1) == 0)
    def _():
        state[...] = jnp.zeros_like(state)
        i = lax.broadcasted_iota(jnp.int32, (c, c), 0)
        j = lax.broadcasted_iota(jnp.int32, (c, c), 1)
        diff = (i - j).astype(F32)
        decay[...] = jnp.where(diff >= 0, jnp.exp(lg * jnp.maximum(diff, 0.0)), 0.0)
        r = lax.broadcasted_iota(jnp.int32, (c, RET_D), 0).astype(F32)
        qdec[...] = jnp.exp(lg * (r + 1.0))
        kdec[...] = jnp.exp(lg * (c - 1.0 - r))

    half = RET_D // 2
    cdec = jnp.exp(jnp.zeros((1, RET_D), F32) + lg * c)
    st = state[...]
    for ci in range(RET_CHUNKS_PER_STEP):
        rs = slice(ci * c, (ci + 1) * c)
        cos = cos_ref[rs, :]
        sin = sin_ref[rs, :]

        def rot(t):
            t1, t2 = t[:, :half], t[:, half:]
            return jnp.concatenate([t1 * cos - t2 * sin, t1 * sin + t2 * cos], axis=1)

        q = rot(q_ref[rs, :].astype(F32))
        k = rot(k_ref[rs, :].astype(F32)) * (RET_D ** -0.5)
        vb = v_ref[rs, :].astype(BF16)
        qb = q.astype(BF16)
        scores = lax.dot_general(qb, k.astype(BF16), NT_DIMS, preferred_element_type=F32) * decay[...]
        o = (jnp.dot(scores.astype(BF16), vb, preferred_element_type=F32)
             + jnp.dot(qb, st.astype(BF16), preferred_element_type=F32) * qdec[...])
        kd = (k * kdec[...]).astype(BF16)
        st = st * cdec + lax.dot_general(kd, vb, TN_DIMS, preferred_element_type=F32)

        mu = jnp.mean(o, axis=-1, keepdims=True)
        oc = o - mu
        var = jnp.mean(oc * oc, axis=-1, keepdims=True)
        y = oc * lax.rsqrt(var + EPS) * gain_ref[...]
        o_ref[rs, :] = (_silu(g_ref[rs, :].astype(F32)) * y).astype(o_ref.dtype)
    state[...] = st


def retention_heads(proj, cos, sin, gain):
    s = proj.shape[0]
    c = RET_CHUNK
    log_gamma = np.log1p(-np.exp2(-5.0 - np.arange(RET_HEADS, dtype=np.float32))).astype(np.float32)
    col = lambda base: (lambda h, n, lg: (n, base + h))
    rows = min(c * RET_CHUNKS_PER_STEP, s)
    grid_spec = pltpu.PrefetchScalarGridSpec(
        num_scalar_prefetch=1,
        grid=(RET_HEADS, s // rows),
        in_specs=[pl.BlockSpec((rows, RET_D), col(0)),
                  pl.BlockSpec((rows, RET_D), col(RET_HEADS)),
                  pl.BlockSpec((rows, RET_D), col(2 * RET_HEADS)),
                  pl.BlockSpec((rows, RET_D), col(3 * RET_HEADS)),
                  pl.BlockSpec((rows, RET_D // 2), lambda h, n, lg: (n, 0)),
                  pl.BlockSpec((rows, RET_D // 2), lambda h, n, lg: (n, 0)),
                  pl.BlockSpec((1, RET_D), lambda h, n, lg: (0, h))],
        out_specs=pl.BlockSpec((rows, RET_D), lambda h, n, lg: (n, h)),
        scratch_shapes=[pltpu.VMEM((RET_D, RET_D), F32),
                        pltpu.VMEM((c, c), F32),
                        pltpu.VMEM((c, RET_D), F32),
                        pltpu.VMEM((c, RET_D), F32)],
    )
    return pl.pallas_call(
        _ret_kernel,
        grid_spec=grid_spec,
        out_shape=jax.ShapeDtypeStruct((s, RET_WIDTH + GLA_WIDTH), BF16),
        compiler_params=_params("arbitrary", "arbitrary"),
        name="retention",
    )(jnp.asarray(log_gamma), proj, proj, proj, proj, cos, sin, gain.reshape(1, RET_WIDTH))


def _split3(x):
    hi = x.astype(BF16)
    r1 = x - hi.astype(F32)
    mid = r1.astype(BF16)
    lo = (r1 - mid.astype(F32)).astype(BF16)
    return hi, mid, lo


def _gla_kernel(q_ref, k_ref, v_ref, g_ref, a_ref, wup_ref, b_ref, gain_ref, mixed_ref, o_ref,
                state_t, pair_level, tri, attn_s):
    del mixed_ref
    c = GLA_CHUNK
    log2e = 1.4426950408889634

    @pl.when(pl.program_id(1) == 0)
    def _():
        state_t[...] = jnp.zeros_like(state_t)
        i = lax.broadcasted_iota(jnp.int32, (c, c), 0)
        j = lax.broadcasted_iota(jnp.int32, (c, c), 1)
        tri[...] = jnp.where(i >= j, 1.0, 0.0).astype(BF16)
        x = jnp.where(i > j, jnp.bitwise_xor(i, j), 0)
        lvl = jnp.full((c, c), GLA_LEVELS[-1] - 1, jnp.int32)
        for p in GLA_LEVELS:
            lvl = lvl + jnp.where(x >= (1 << p), 1, 0)
        pair_level[...] = lvl

    a_hi, a_mid, a_lo = _split3(a_ref[...])
    w_hi, w_mid, w_lo = _split3(wup_ref[...])
    dotf = lambda x, y: jnp.dot(x, y, preferred_element_type=F32)
    z = (dotf(a_hi, w_hi) + (dotf(a_hi, w_mid) + dotf(a_mid, w_hi))
         + (dotf(a_hi, w_lo) + dotf(a_lo, w_hi) + dotf(a_mid, w_mid))) + b_ref[...]
    la = (jnp.minimum(z, 0.0) - jnp.log1p(jnp.exp(-jnp.abs(z)))) * (log2e / GLA_GATE_TAU)
    l_hi, l_mid, l_lo = _split3(la)
    t = tri[...]
    cum = dotf(t, l_hi) + dotf(t, l_mid) + dotf(t, l_lo)

    qs = q_ref[...].astype(F32) * (GLA_DK ** -0.5)
    k = k_ref[...].astype(F32)
    vb = v_ref[...].astype(BF16)
    st = state_t[...]

    o = lax.dot_general((qs * jnp.exp2(cum)).astype(BF16), st.astype(BF16), NT_DIMS,
                        preferred_element_type=F32)

    attn_s[...] = jnp.zeros_like(attn_s)
    for p in GLA_LEVELS:
        half = 1 << p
        nb = c // (2 * half)
        q_rows, k_rows = [], []
        for blk in range(nb):
            up = slice(blk * 2 * half, blk * 2 * half + half)
            lo = slice(blk * 2 * half + half, (blk + 1) * 2 * half)
            bnd = cum[up.stop - 1:up.stop, :]
            e_lo = jnp.exp2(cum[lo] - bnd)
            k_rows += [k[up] * jnp.exp2(bnd - cum[up]), k[lo] * e_lo]
            q_rows.append(qs[lo] * e_lo)
        q_t = q_rows[0] if nb == 1 else jnp.concatenate(q_rows, axis=0)
        k_t = jnp.concatenate(k_rows, axis=0)
        part = lax.dot_general(q_t.astype(BF16), k_t.astype(BF16), NT_DIMS,
                               preferred_element_type=F32)
        for blk in range(nb):
            lo = slice(blk * 2 * half + half, (blk + 1) * 2 * half)
            attn_s[lo, :] = jnp.where(pair_level[lo, :] == p,
                                      part[blk * half:(blk + 1) * half, :], attn_s[lo, :])

    rows = lax.broadcasted_iota(jnp.int32, (SUBLANES, 1), 0)
    lane = lax.broadcasted_iota(jnp.int32, (SUBLANES, LANES), 1)
    for blk in range(c // SUBLANES):
        r = blk * SUBLANES
        rs = slice(r, r + SUBLANES)
        ls = slice(r // LANES * LANES, r // LANES * LANES + LANES)
        q8, k8, c8 = qs[rs], k[rs], cum[rs]
        tile = attn_s[rs, ls]
        for m in range(SUBLANES):
            w = jnp.exp2(c8 - c8[m:m + 1, :])
            a = jnp.sum(q8 * w * k8[m:m + 1, :], axis=-1, keepdims=True)
            a = jnp.where(rows >= m, a, 0.0)
            tile = jnp.where(lane == r % LANES + m, a, tile)
        attn_s[rs, ls] = tile
    o = o + jnp.dot(attn_s[...].astype(BF16), vb, preferred_element_type=F32)

    last = cum[c - 1:c, :]
    kd = (k * jnp.exp2(last - cum)).astype(BF16)
    state_t[...] = st * jnp.exp2(last) + lax.dot_general(vb, kd, TN_DIMS, preferred_element_type=F32)

    ms = jnp.mean(o * o, axis=-1, keepdims=True)
    y = o * lax.rsqrt(ms + EPS) * gain_ref[...]
    o_ref[...] = (_silu(g_ref[...].astype(F32)) * y).astype(o_ref.dtype)


def gla_heads(proj, ga, w_up, b, gain, mixed):
    s = proj.shape[0]
    c = GLA_CHUNK
    qk_base = 4 * RET_WIDTH // GLA_DK
    v_base = (4 * RET_WIDTH + 2 * GLA_KEY_WIDTH) // GLA_DV
    w_up_pad = jnp.zeros((GATE_PAD, GLA_KEY_WIDTH), F32).at[:GLA_GATE_RANK].set(w_up)
    return pl.pallas_call(
        _gla_kernel,
        grid=(GLA_HEADS, s // c),
        in_specs=[pl.BlockSpec((c, GLA_DK), lambda h, n: (n, qk_base + h)),
                  pl.BlockSpec((c, GLA_DK), lambda h, n: (n, qk_base + GLA_HEADS + h)),
                  pl.BlockSpec((c, GLA_DV), lambda h, n: (n, v_base + h)),
                  pl.BlockSpec((c, GLA_DV), lambda h, n: (n, v_base + GLA_HEADS + h)),
                  pl.BlockSpec((c, GATE_PAD), lambda h, n: (n, 0)),
                  pl.BlockSpec((GATE_PAD, GLA_DK), lambda h, n: (0, h)),
                  pl.BlockSpec((1, GLA_DK), lambda h, n: (0, h)),
                  pl.BlockSpec((1, GLA_DV), lambda h, n: (0, h)),
                  pl.BlockSpec(memory_space=pl.ANY)],
        out_specs=pl.BlockSpec((c, GLA_DV), lambda h, n: (n, RET_WIDTH // GLA_DV + h)),
        out_shape=jax.ShapeDtypeStruct(mixed.shape, mixed.dtype),
        input_output_aliases={8: 0},
        scratch_shapes=[pltpu.VMEM((GLA_DV, GLA_DK), F32),
                        pltpu.VMEM((c, c), jnp.int32),
                        pltpu.VMEM((c, c), BF16),
                        pltpu.VMEM((c, c), F32)],
        compiler_params=_params("arbitrary", "arbitrary"),
        name="gla",
    )(proj, proj, proj, proj, ga, w_up_pad, b.reshape(1, GLA_KEY_WIDTH), gain.reshape(1, GLA_WIDTH),
      mixed)


def kernel(x, positions, mix_norm, w_in, gla_w_up, gla_b, ret_gain, gla_gain, w_out, ffn_norm,
           w_gate, w_up, w_down, final_norm):
    b, s, d = x.shape
    x = x.reshape(b * s, d)
    cos, sin = rope_tables(positions.reshape(b * s))
    w_in_t = jnp.swapaxes(w_in, 1, 2)
    for l in range(DEPTH):
        h = rmsnorm(x, mix_norm[l], BF16)
        proj, ga, wd = in_proj(h, w_in_t, w_down, l)
        mixed = retention_heads(proj, cos, sin, ret_gain[l])
        mixed = gla_heads(proj, ga, gla_w_up[l], gla_b[l], gla_gain[l], mixed)
        x = matmul_resid(mixed, w_out, x, layer=l, name="out_proj")
        h = rmsnorm(x, ffn_norm[l], BF16)
        hid = ffn_up(h, w_gate, w_up, l)
        x = matmul_resid(hid, wd, x, tm=512, tn=512, single_buffer_a=False, name="ffn_down")
    return rmsnorm(x, final_norm, F32).reshape(b, s, d)
```

```python
import functools

import numpy as np
import jax
import jax.numpy as jnp
from jax import lax
from jax.experimental import pallas as pl
from jax.experimental.pallas import tpu as pltpu

F32 = jnp.float32
BF16 = jnp.bfloat16

D_MODEL = 4096
DEPTH = 2
RET_HEADS = 8
RET_D = 256
RET_WIDTH = RET_HEADS * RET_D
GLA_HEADS = 4
GLA_DK = 256
GLA_DV = 512
GLA_KEY_WIDTH = GLA_HEADS * GLA_DK
GLA_WIDTH = GLA_HEADS * GLA_DV
GLA_GATE_RANK = 16
GLA_GATE_TAU = 16.0
FFN_HIDDEN = 11008
ROPE_BASE = 10000.0
EPS = 1e-6

MAIN_WIDTH = 4 * RET_WIDTH + 2 * GLA_KEY_WIDTH + 2 * GLA_WIDTH
LANES = 128
SUBLANES = 8
GATE_PAD = LANES

RET_CHUNK = 256
RET_CHUNKS_PER_STEP = 4
GLA_CHUNK = 256
GLA_LEVELS = (7, 6, 5, 4, 3)

VMEM_LIMIT = 62 * 1024 * 1024

NT_DIMS = (((1,), (1,)), ((), ()))
TN_DIMS = (((0,), (0,)), ((), ()))


def _params(*sem):
    return pltpu.CompilerParams(dimension_semantics=sem, vmem_limit_bytes=VMEM_LIMIT)


def _silu(x):
    return x * (1.0 / (1.0 + jnp.exp(-x)))


def _rmsnorm_kernel(x_ref, g_ref, o_ref):
    x = x_ref[...]
    ms = jnp.mean(x * x, axis=-1, keepdims=True)
    o_ref[...] = (x * lax.rsqrt(ms + EPS) * g_ref[...]).astype(o_ref.dtype)


def rmsnorm(x, gain, out_dtype, tm=256):
    s, d = x.shape
    return pl.pallas_call(
        _rmsnorm_kernel,
        grid=(s // tm,),
        in_specs=[pl.BlockSpec((tm, d), lambda i: (i, 0)),
                  pl.BlockSpec((1, d), lambda i: (0, 0))],
        out_specs=pl.BlockSpec((tm, d), lambda i: (i, 0)),
        out_shape=jax.ShapeDtypeStruct((s, d), out_dtype),
        compiler_params=_params("arbitrary"),
        name="rmsnorm",
    )(x, gain.reshape(1, d))


def _fold_lanes(v):
    acc = v[:, :LANES]
    for g in range(1, v.shape[1] // LANES):
        acc = acc + v[:, g * LANES:(g + 1) * LANES]
    return acc


def _row_scale(ss):
    return lax.rsqrt(jnp.sum(ss, axis=-1, keepdims=True) * (1.0 / D_MODEL) + EPS)


def _norm_inputs_kernel(x_ref, xb_ref, ss_ref):
    x = x_ref[...]
    xb_ref[...] = x.astype(BF16)
    ss_ref[...] = _fold_lanes(x * x)


def norm_inputs(x, tm=256):
    s, d = x.shape
    return pl.pallas_call(
        _norm_inputs_kernel,
        grid=(s // tm,),
        in_specs=[pl.BlockSpec((tm, d), lambda i: (i, 0))],
        out_specs=[pl.BlockSpec((tm, d), lambda i: (i, 0)),
                   pl.BlockSpec((tm, LANES), lambda i: (i, 0))],
        out_shape=[jax.ShapeDtypeStruct((s, d), BF16),
                   jax.ShapeDtypeStruct((s, LANES), F32)],
        compiler_params=_params("arbitrary"),
        name="norm_inputs",
    )(x)


def _row_block_spec(tm, k, single_buffer=True):
    return pl.BlockSpec((tm, k), lambda i, j: (i, 0),
                        pipeline_mode=pl.Buffered(1) if single_buffer else None)


def _mm_resid_kernel(a_ref, w_ref, r_ref, o_ref, *norm_refs):
    x = jnp.dot(a_ref[...], w_ref[...].astype(BF16), preferred_element_type=F32) + r_ref[...]
    o_ref[...] = x
    if norm_refs:
        xb_ref, ss_ref = norm_refs
        xb_ref[...] = x.astype(BF16)
        part = _fold_lanes(x * x)
        j = pl.program_id(1)

        @pl.when(j == 0)
        def _():
            ss_ref[...] = part

        @pl.when(j > 0)
        def _():
            ss_ref[...] += part


def matmul_resid(a, w, resid, layer=None, tm=1024, tn=512, single_buffer_a=False,
                 emit_norm_inputs=True, name="matmul"):
    m, k = a.shape
    n = w.shape[-1]
    if w.ndim == 3:
        w_spec = pl.BlockSpec((None, k, tn), lambda i, j: (layer, 0, j))
    else:
        w_spec = pl.BlockSpec((k, tn), lambda i, j: (0, j))
    tile = pl.BlockSpec((tm, tn), lambda i, j: (i, j))
    out_specs, out_shape = [tile], [jax.ShapeDtypeStruct((m, n), F32)]
    if emit_norm_inputs:
        out_specs += [tile, pl.BlockSpec((tm, LANES), lambda i, j: (i, 0))]
        out_shape += [jax.ShapeDtypeStruct((m, n), BF16), jax.ShapeDtypeStruct((m, LANES), F32)]
    outs = pl.pallas_call(
        _mm_resid_kernel,
        grid=(m // tm, n // tn),
        in_specs=[_row_block_spec(tm, k, single_buffer_a), w_spec, tile],
        out_specs=out_specs,
        out_shape=out_shape,
        compiler_params=_params("arbitrary", "arbitrary"),
        name=name,
    )(a, w, resid)
    return outs if emit_norm_inputs else outs[0]


def _in_proj_kernel(xb_ref, ss_ref, gain_ref, w_ref, wlr_ref, wd_ref, o_ref, ga_ref, wd_bf16_ref):
    w = (w_ref[...] * gain_ref[...]).astype(BF16)
    acc = lax.dot_general(xb_ref[...], w, NT_DIMS, preferred_element_type=F32)
    o_ref[...] = (acc * _row_scale(ss_ref[...])).astype(o_ref.dtype)
    wd_bf16_ref[...] = wd_ref[...].astype(BF16)

    @pl.when(pl.program_id(1) == 0)
    def _():
        wlr = wlr_ref[...]
        row = lax.broadcasted_iota(jnp.int32, wlr.shape, 0)
        wlr = jnp.where(row < GLA_GATE_RANK, wlr * gain_ref[...], 0.0)
        ga = lax.dot_general(xb_ref[...], wlr.astype(BF16), NT_DIMS, preferred_element_type=F32)
        ga_ref[...] = ga * _row_scale(ss_ref[...])


def in_proj(xb, ss, gain, w_in_t, w_down, layer, tm=2048, tn=512, tr=128):
    m, k = xb.shape
    _, f, d = w_down.shape
    nj = MAIN_WIDTH // tn
    n_wd = f // tr
    assert n_wd * tr == f and n_wd <= (m // tm) * nj
    wd_block = lambda i, j: jnp.minimum(i * nj + j, n_wd - 1)
    return pl.pallas_call(
        _in_proj_kernel,
        grid=(m // tm, nj),
        in_specs=[_row_block_spec(tm, k),
                  _row_block_spec(tm, LANES),
                  pl.BlockSpec((1, k), lambda i, j: (0, 0)),
                  pl.BlockSpec((None, tn, k), lambda i, j: (layer, j, 0)),
                  pl.BlockSpec((None, GATE_PAD, k), lambda i, j: (layer, MAIN_WIDTH // GATE_PAD, 0)),
                  pl.BlockSpec((None, tr, d), lambda i, j: (layer, wd_block(i, j), 0))],
        out_specs=[pl.BlockSpec((tm, tn), lambda i, j: (i, j)),
                   pl.BlockSpec((tm, GATE_PAD), lambda i, j: (i, 0)),
                   pl.BlockSpec((tr, d), lambda i, j: (wd_block(i, j), 0))],
        out_shape=[jax.ShapeDtypeStruct((m, MAIN_WIDTH), BF16),
                   jax.ShapeDtypeStruct((m, GATE_PAD), F32),
                   jax.ShapeDtypeStruct((f, d), BF16)],
        compiler_params=_params("arbitrary", "arbitrary"),
        name="in_proj",
    )(xb, ss, gain.reshape(1, k), w_in_t, w_in_t, w_down)


def _ffn_up_kernel(xb_ref, ss_ref, gain_ref, wg_ref, wu_ref, o_ref, *, row_splits):
    gain = jnp.concatenate([gain_ref[...]] * (wg_ref.shape[1] // LANES), axis=1)
    wg = (wg_ref[...] * gain).astype(BF16)
    wu = (wu_ref[...] * gain).astype(BF16)
    rows = xb_ref.shape[0] // row_splits
    for r in range(row_splits):
        rs = slice(r * rows, (r + 1) * rows)
        scale = _row_scale(ss_ref[rs, :])
        g = jnp.dot(xb_ref[rs, :], wg, preferred_element_type=F32) * scale
        u = jnp.dot(xb_ref[rs, :], wu, preferred_element_type=F32) * scale
        o_ref[rs, :] = (_silu(g) * u).astype(o_ref.dtype)


def ffn_up(xb, ss, gain, w_gate, w_up, layer, tm=4096, tn=256, row_splits=4):
    m, k = xb.shape
    n = w_gate.shape[2]
    w_spec = pl.BlockSpec((None, k, tn), lambda i, j: (layer, 0, j))
    gain_rows = jnp.broadcast_to(gain[:, None], (k, LANES))
    return pl.pallas_call(
        functools.partial(_ffn_up_kernel, row_splits=row_splits),
        grid=(m // tm, n // tn),
        in_specs=[_row_block_spec(tm, k),
                  _row_block_spec(tm, LANES),
                  pl.BlockSpec((k, LANES), lambda i, j: (0, 0), pipeline_mode=pl.Buffered(1)),
                  w_spec, w_spec],
        out_specs=pl.BlockSpec((tm, tn), lambda i, j: (i, j)),
        out_shape=jax.ShapeDtypeStruct((m, n), BF16),
        compiler_params=_params("arbitrary", "arbitrary"),
        name="ffn_up",
    )(xb, ss, gain_rows, w_gate, w_up)


def _rope_kernel(pos_ref, inv_ref, cos_ref, sin_ref):
    ang = pos_ref[...].astype(F32) * inv_ref[...]
    cos_ref[...] = jnp.cos(ang)
    sin_ref[...] = jnp.sin(ang)


def rope_tables(positions, tm=1024):
    s = positions.shape[0]
    tm = min(tm, s)
    half = RET_D // 2
    inv_freq = (ROPE_BASE ** (-np.arange(half, dtype=np.float32) / half)).astype(np.float32)
    return pl.pallas_call(
        _rope_kernel,
        grid=(s // tm,),
        in_specs=[pl.BlockSpec((tm, 1), lambda i: (i, 0)),
                  pl.BlockSpec((1, half), lambda i: (0, 0))],
        out_specs=[pl.BlockSpec((tm, half), lambda i: (i, 0))] * 2,
        out_shape=[jax.ShapeDtypeStruct((s, half), F32)] * 2,
        compiler_params=_params("arbitrary"),
        name="rope_tables",
    )(positions.reshape(s, 1), jnp.asarray(inv_freq).reshape(1, half))


def _ret_kernel(lg_ref, q_ref, k_ref, v_ref, g_ref, cos_ref, sin_ref, gain_ref, o_ref,
                state, decay, qdec, kdec):
    c = RET_CHUNK
    lg = lg_ref[pl.program_id(0)]

    @pl.when(pl.program_id(1) == 0)
    def _():
        state[...] = jnp.zeros_like(state)
        i = lax.broadcasted_iota(jnp.int32, (c, c), 0)
        j = lax.broadcasted_iota(jnp.int32, (c, c), 1)
        diff = (i - j).astype(F32)
        decay[...] = jnp.where(diff >= 0, jnp.exp(lg * jnp.maximum(diff, 0.0)), 0.0)
        r = lax.broadcasted_iota(jnp.int32, (c, RET_D), 0).astype(F32)
        qdec[...] = jnp.exp(lg * (r + 1.0))
        kdec[...] = jnp.exp(lg * (c - 1.0 - r))

    half = RET_D // 2
    cdec = jnp.exp(jnp.zeros((1, RET_D), F32) + lg * c)
    st = state[...]
    for ci in range(RET_CHUNKS_PER_STEP):
        rs = slice(ci * c, (ci + 1) * c)
        cos = cos_ref[rs, :]
        sin = sin_ref[rs, :]

        def rot(t):
            t1, t2 = t[:, :half], t[:, half:]
            return jnp.concatenate([t1 * cos - t2 * sin, t1 * sin + t2 * cos], axis=1)

        q = rot(q_ref[rs, :].astype(F32))
        k = rot(k_ref[rs, :].astype(F32)) * (RET_D ** -0.5)
        vb = v_ref[rs, :].astype(BF16)
        qb = q.astype(BF16)
        scores = lax.dot_general(qb, k.astype(BF16), NT_DIMS, preferred_element_type=F32) * decay[...]
        o = (jnp.dot(scores.astype(BF16), vb, preferred_element_type=F32)
             + jnp.dot(qb, st.astype(BF16), preferred_element_type=F32) * qdec[...])
        kd = (k * kdec[...]).astype(BF16)
        st = st * cdec + lax.dot_general(kd, vb, TN_DIMS, preferred_element_type=F32)

        mu = jnp.mean(o, axis=-1, keepdims=True)
        oc = o - mu
        var = jnp.mean(oc * oc, axis=-1, keepdims=True)
        y = oc * lax.rsqrt(var + EPS) * gain_ref[...]
        o_ref[rs, :] = (_silu(g_ref[rs, :].astype(F32)) * y).astype(o_ref.dtype)
    state[...] = st


def retention_heads(proj, cos, sin, gain):
    s = proj.shape[0]
    c = RET_CHUNK
    log_gamma = np.log1p(-np.exp2(-5.0 - np.arange(RET_HEADS, dtype=np.float32))).astype(np.float32)
    col = lambda base: (lambda h, n, lg: (n, base + h))
    rows = min(c * RET_CHUNKS_PER_STEP, s)
    grid_spec = pltpu.PrefetchScalarGridSpec(
        num_scalar_prefetch=1,
        grid=(RET_HEADS, s // rows),
        in_specs=[pl.BlockSpec((rows, RET_D), col(0)),
                  pl.BlockSpec((rows, RET_D), col(RET_HEADS)),
                  pl.BlockSpec((rows, RET_D), col(2 * RET_HEADS)),
                  pl.BlockSpec((rows, RET_D), col(3 * RET_HEADS)),
                  pl.BlockSpec((rows, RET_D // 2), lambda h, n, lg: (n, 0)),
                  pl.BlockSpec((rows, RET_D // 2), lambda h, n, lg: (n, 0)),
                  pl.BlockSpec((1, RET_D), lambda h, n, lg: (0, h))],
        out_specs=pl.BlockSpec((rows, RET_D), lambda h, n, lg: (n, h)),
        scratch_shapes=[pltpu.VMEM((RET_D, RET_D), F32),
                        pltpu.VMEM((c, c), F32),
                        pltpu.VMEM((c, RET_D), F32),
                        pltpu.VMEM((c, RET_D), F32)],
    )
    return pl.pallas_call(
        _ret_kernel,
        grid_spec=grid_spec,
        out_shape=jax.ShapeDtypeStruct((s, RET_WIDTH + GLA_WIDTH), BF16),
        compiler_params=_params("arbitrary", "arbitrary"),
        name="retention",
    )(jnp.asarray(log_gamma), proj, proj, proj, proj, cos, sin, gain.reshape(1, RET_WIDTH))


def _split3(x):
    hi = x.astype(BF16)
    r1 = x - hi.astype(F32)
    mid = r1.astype(BF16)
    lo = (r1 - mid.astype(F32)).astype(BF16)
    return hi, mid, lo


def _gla_kernel(q_ref, k_ref, v_ref, g_ref, a_ref, wup_ref, b_ref, gain_ref, mixed_ref, o_ref,
                state_t, pair_level, tri, attn_s):
    del mixed_ref
    c = GLA_CHUNK
    log2e = 1.4426950408889634

    @pl.when(pl.program_id(1) == 0)
    def _():
        state_t[...] = jnp.zeros_like(state_t)
        i = lax.broadcasted_iota(jnp.int32, (c, c), 0)
        j = lax.broadcasted_iota(jnp.int32, (c, c), 1)
        tri[...] = jnp.where(i >= j, 1.0, 0.0).astype(BF16)
        x = jnp.where(i > j, jnp.bitwise_xor(i, j), 0)
        lvl = jnp.full((c, c), GLA_LEVELS[-1] - 1, jnp.int32)
        for p in GLA_LEVELS:
            lvl = lvl + jnp.where(x >= (1 << p), 1, 0)
        pair_level[...] = lvl

    a_hi, a_mid, a_lo = _split3(a_ref[...])
    w_hi, w_mid, w_lo = _split3(wup_ref[...])
    dotf = lambda x, y: jnp.dot(x, y, preferred_element_type=F32)
    z = (dotf(a_hi, w_hi) + (dotf(a_hi, w_mid) + dotf(a_mid, w_hi))
         + (dotf(a_hi, w_lo) + dotf(a_lo, w_hi) + dotf(a_mid, w_mid))) + b_ref[...]
    la = (jnp.minimum(z, 0.0) - jnp.log1p(jnp.exp(-jnp.abs(z)))) * (log2e / GLA_GATE_TAU)
    l_hi, l_mid, l_lo = _split3(la)
    t = tri[...]
    cum = dotf(t, l_hi) + dotf(t, l_mid) + dotf(t, l_lo)

    qs = q_ref[...].astype(F32) * (GLA_DK ** -0.5)
    k = k_ref[...].astype(F32)
    vb = v_ref[...].astype(BF16)
    st = state_t[...]

    o = lax.dot_general((qs * jnp.exp2(cum)).astype(BF16), st.astype(BF16), NT_DIMS,
                        preferred_element_type=F32)

    attn_s[...] = jnp.zeros_like(attn_s)
    for p in GLA_LEVELS:
        half = 1 << p
        nb = c // (2 * half)
        q_rows, k_rows = [], []
        for blk in range(nb):
            up = slice(blk * 2 * half, blk * 2 * half + half)
            lo = slice(blk * 2 * half + half, (blk + 1) * 2 * half)
            bnd = cum[up.stop - 1:up.stop, :]
            e_lo = jnp.exp2(cum[lo] - bnd)
            k_rows += [k[up] * jnp.exp2(bnd - cum[up]), k[lo] * e_lo]
            q_rows.append(qs[lo] * e_lo)
        q_t = q_rows[0] if nb == 1 else jnp.concatenate(q_rows, axis=0)
        k_t = jnp.concatenate(k_rows, axis=0)
        part = lax.dot_general(q_t.astype(BF16), k_t.astype(BF16), NT_DIMS,
                               preferred_element_type=F32)
        for blk in range(nb):
            lo = slice(blk * 2 * half + half, (blk + 1) * 2 * half)
            attn_s[lo, :] = jnp.where(pair_level[lo, :] == p,
                                      part[blk * half:(blk + 1) * half, :], attn_s[lo, :])

    rows = lax.broadcasted_iota(jnp.int32, (SUBLANES, 1), 0)
    lane = lax.broadcasted_iota(jnp.int32, (SUBLANES, LANES), 1)
    for blk in range(c // SUBLANES):
        r = blk * SUBLANES
        rs = slice(r, r + SUBLANES)
        ls = slice(r // LANES * LANES, r // LANES * LANES + LANES)
        q8, k8, c8 = qs[rs], k[rs], cum[rs]
        tile = attn_s[rs, ls]
        for m in range(SUBLANES):
            w = jnp.exp2(c8 - c8[m:m + 1, :])
            a = jnp.sum(q8 * w * k8[m:m + 1, :], axis=-1, keepdims=True)
            a = jnp.where(rows >= m, a, 0.0)
            tile = jnp.where(lane == r % LANES + m, a, tile)
        attn_s[rs, ls] = tile
    o = o + jnp.dot(attn_s[...].astype(BF16), vb, preferred_element_type=F32)

    last = cum[c - 1:c, :]
    kd = (k * jnp.exp2(last - cum)).astype(BF16)
    state_t[...] = st * jnp.exp2(last) + lax.dot_general(vb, kd, TN_DIMS, preferred_element_type=F32)

    ms = jnp.mean(o * o, axis=-1, keepdims=True)
    y = o * lax.rsqrt(ms + EPS) * gain_ref[...]
    o_ref[...] = (_silu(g_ref[...].astype(F32)) * y).astype(o_ref.dtype)


def gla_heads(proj, ga, w_up, b, gain, mixed):
    s = proj.shape[0]
    c = GLA_CHUNK
    qk_base = 4 * RET_WIDTH // GLA_DK
    v_base = (4 * RET_WIDTH + 2 * GLA_KEY_WIDTH) // GLA_DV
    w_up_pad = jnp.zeros((GATE_PAD, GLA_KEY_WIDTH), F32).at[:GLA_GATE_RANK].set(w_up)
    return pl.pallas_call(
        _gla_kernel,
        grid=(GLA_HEADS, s // c),
        in_specs=[pl.BlockSpec((c, GLA_DK), lambda h, n: (n, qk_base + h)),
                  pl.BlockSpec((c, GLA_DK), lambda h, n: (n, qk_base + GLA_HEADS + h)),
                  pl.BlockSpec((c, GLA_DV), lambda h, n: (n, v_base + h)),
                  pl.BlockSpec((c, GLA_DV), lambda h, n: (n, v_base + GLA_HEADS + h)),
                  pl.BlockSpec((c, GATE_PAD), lambda h, n: (n, 0)),
                  pl.BlockSpec((GATE_PAD, GLA_DK), lambda h, n: (0, h)),
                  pl.BlockSpec((1, GLA_DK), lambda h, n: (0, h)),
                  pl.BlockSpec((1, GLA_DV), lambda h, n: (0, h)),
                  pl.BlockSpec(memory_space=pl.ANY)],
        out_specs=pl.BlockSpec((c, GLA_DV), lambda h, n: (n, RET_WIDTH // GLA_DV + h)),
        out_shape=jax.ShapeDtypeStruct(mixed.shape, mixed.dtype),
        input_output_aliases={8: 0},
        scratch_shapes=[pltpu.VMEM((GLA_DV, GLA_DK), F32),
                        pltpu.VMEM((c, c), jnp.int32),
                        pltpu.VMEM((c, c), BF16),
                        pltpu.VMEM((c, c), F32)],
        compiler_params=_params("arbitrary", "arbitrary"),
        name="gla",
    )(proj, proj, proj, proj, ga, w_up_pad, b.reshape(1, GLA_KEY_WIDTH), gain.reshape(1, GLA_WIDTH),
      mixed)


def kernel(x, positions, mix_norm, w_in, gla_w_up, gla_b, ret_gain, gla_gain, w_out, ffn_norm,
           w_gate, w_up, w_down, final_norm):
    b, s, d = x.shape
    x = x.reshape(b * s, d)
    cos, sin = rope_tables(positions.reshape(b * s))
    w_in_t = jnp.swapaxes(w_in, 1, 2)
    xb, ss = norm_inputs(x)
    for l in range(DEPTH):
        proj, ga, wd = in_proj(xb, ss, mix_norm[l], w_in_t, w_down, l)
        mixed = retention_heads(proj, cos, sin, ret_gain[l])
        mixed = gla_heads(proj, ga, gla_w_up[l], gla_b[l], gla_gain[l], mixed)
        x, xb, ss = matmul_resid(mixed, w_out, x, layer=l, name="out_proj")
        hid = ffn_up(xb, ss, ffn_norm[l], w_gate, w_up, l)
        if l + 1 < DEPTH:
            x, xb, ss = matmul_resid(hid, wd, x, tm=512, name="ffn_down")
        else:
            x = matmul_resid(hid, wd, x, tm=512, emit_norm_inputs=False, name="ffn_down")
    return rmsnorm(x, final_norm, F32).reshape(b, s, d)
```

```python
import functools

import numpy as np
import jax
import jax.numpy as jnp
from jax import lax
from jax.experimental import pallas as pl
from jax.experimental.pallas import tpu as pltpu

F32 = jnp.float32
BF16 = jnp.bfloat16

D_MODEL = 4096
DEPTH = 2
RET_HEADS = 8
RET_D = 256
RET_WIDTH = RET_HEADS * RET_D
GLA_HEADS = 4
GLA_DK = 256
GLA_DV = 512
GLA_KEY_WIDTH = GLA_HEADS * GLA_DK
GLA_WIDTH = GLA_HEADS * GLA_DV
GLA_GATE_RANK = 16
GLA_GATE_TAU = 16.0
FFN_HIDDEN = 11008
ROPE_BASE = 10000.0
EPS = 1e-6

MAIN_WIDTH = 4 * RET_WIDTH + 2 * GLA_KEY_WIDTH + 2 * GLA_WIDTH
LANES = 128
SUBLANES = 8
GATE_PAD = LANES

RET_CHUNK = 256
RET_CHUNKS_PER_STEP = 4
GLA_CHUNK = 256
GLA_LEVELS = (7, 6, 5, 4, 3)

VMEM_LIMIT = 62 * 1024 * 1024

NT_DIMS = (((1,), (1,)), ((), ()))
TN_DIMS = (((0,), (0,)), ((), ()))


def _params(*sem):
    return pltpu.CompilerParams(dimension_semantics=sem, vmem_limit_bytes=VMEM_LIMIT)


def _silu(x):
    return x * (1.0 / (1.0 + jnp.exp(-x)))


def _rmsnorm_kernel(x_ref, g_ref, o_ref):
    x = x_ref[...]
    ms = jnp.mean(x * x, axis=-1, keepdims=True)
    o_ref[...] = (x * lax.rsqrt(ms + EPS) * g_ref[...]).astype(o_ref.dtype)


def rmsnorm(x, gain, out_dtype, tm=256):
    s, d = x.shape
    return pl.pallas_call(
        _rmsnorm_kernel,
        grid=(s // tm,),
        in_specs=[pl.BlockSpec((tm, d), lambda i: (i, 0)),
                  pl.BlockSpec((1, d), lambda i: (0, 0))],
        out_specs=pl.BlockSpec((tm, d), lambda i: (i, 0)),
        out_shape=jax.ShapeDtypeStruct((s, d), out_dtype),
        compiler_params=_params("arbitrary"),
        name="rmsnorm",
    )(x, gain.reshape(1, d))


def _fold_lanes(v):
    acc = v[:, :LANES]
    for g in range(1, v.shape[1] // LANES):
        acc = acc + v[:, g * LANES:(g + 1) * LANES]
    return acc


def _row_scale(ss):
    r = lax.rsqrt(jnp.sum(ss, axis=-1, keepdims=True) * (1.0 / D_MODEL) + EPS)
    return jnp.broadcast_to(r, ss.shape)


def _scale_rows(acc, r):
    return acc * jnp.concatenate([r] * (acc.shape[1] // LANES), axis=1)


def _norm_inputs_kernel(x_ref, g_ref, xg_ref, r_ref):
    x = x_ref[...]
    xg_ref[...] = (x * g_ref[...]).astype(BF16)
    r_ref[...] = _row_scale(_fold_lanes(x * x))


def norm_inputs(x, gain, tm=256):
    s, d = x.shape
    return pl.pallas_call(
        _norm_inputs_kernel,
        grid=(s // tm,),
        in_specs=[pl.BlockSpec((tm, d), lambda i: (i, 0)),
                  pl.BlockSpec((1, d), lambda i: (0, 0))],
        out_specs=[pl.BlockSpec((tm, d), lambda i: (i, 0)),
                   pl.BlockSpec((tm, LANES), lambda i: (i, 0))],
        out_shape=[jax.ShapeDtypeStruct((s, d), BF16),
                   jax.ShapeDtypeStruct((s, LANES), F32)],
        compiler_params=_params("arbitrary"),
        name="norm_inputs",
    )(x, gain.reshape(1, d))


def _row_block_spec(tm, k, single_buffer=True):
    return pl.BlockSpec((tm, k), lambda i, j: (i, 0),
                        pipeline_mode=pl.Buffered(1) if single_buffer else None)


def _mm_resid_kernel(a_ref, w_ref, res_ref, *refs):
    o_ref = refs[-1] if len(refs) == 1 else refs[1]
    x = jnp.dot(a_ref[...], w_ref[...].astype(BF16), preferred_element_type=F32) + res_ref[...]
    o_ref[...] = x
    if len(refs) > 1:
        g_ref, _, xg_ref, r_ref = refs
        xg_ref[...] = (x * g_ref[...]).astype(BF16)
        part = _fold_lanes(x * x)
        j = pl.program_id(1)

        @pl.when(j == 0)
        def _():
            r_ref[...] = part

        @pl.when(j > 0)
        def _():
            r_ref[...] += part

        @pl.when(j == pl.num_programs(1) - 1)
        def _():
            r_ref[...] = _row_scale(r_ref[...])


def matmul_resid(a, w, resid, layer=None, next_gain=None, tm=1024, tn=512, name="matmul"):
    m, k = a.shape
    n = w.shape[-1]
    if w.ndim == 3:
        w_spec = pl.BlockSpec((None, k, tn), lambda i, j: (layer, 0, j))
    else:
        w_spec = pl.BlockSpec((k, tn), lambda i, j: (0, j))
    tile = pl.BlockSpec((tm, tn), lambda i, j: (i, j))
    in_specs, args = [_row_block_spec(tm, k, False), w_spec, tile], [a, w, resid]
    out_specs, out_shape = [tile], [jax.ShapeDtypeStruct((m, n), F32)]
    if next_gain is not None:
        in_specs.append(pl.BlockSpec((1, tn), lambda i, j: (0, j)))
        args.append(next_gain.reshape(1, n))
        out_specs += [tile, pl.BlockSpec((tm, LANES), lambda i, j: (i, 0))]
        out_shape += [jax.ShapeDtypeStruct((m, n), BF16), jax.ShapeDtypeStruct((m, LANES), F32)]
    outs = pl.pallas_call(
        _mm_resid_kernel,
        grid=(m // tm, n // tn),
        in_specs=in_specs,
        out_specs=out_specs,
        out_shape=out_shape,
        compiler_params=_params("arbitrary", "arbitrary"),
        name=name,
    )(*args)
    return outs if next_gain is not None else outs[0]


def _in_proj_kernel(xg_ref, r_ref, w_ref, wlr_ref, wd_ref, o_ref, ga_ref, wd_bf16_ref):
    acc = lax.dot_general(xg_ref[...], w_ref[...].astype(BF16), NT_DIMS, preferred_element_type=F32)
    o_ref[...] = _scale_rows(acc, r_ref[...]).astype(o_ref.dtype)
    wd_bf16_ref[...] = wd_ref[...].astype(BF16)

    @pl.when(pl.program_id(1) == 0)
    def _():
        wlr = wlr_ref[...]
        row = lax.broadcasted_iota(jnp.int32, wlr.shape, 0)
        wlr = jnp.where(row < GLA_GATE_RANK, wlr, 0.0)
        ga = lax.dot_general(xg_ref[...], wlr.astype(BF16), NT_DIMS, preferred_element_type=F32)
        ga_ref[...] = ga * r_ref[...]


def in_proj(xg, r, w_in_t, w_down, layer, tm=2048, tn=512, tr=128):
    m, k = xg.shape
    _, f, d = w_down.shape
    nj = MAIN_WIDTH // tn
    n_wd = f // tr
    assert n_wd * tr == f and n_wd <= (m // tm) * nj
    wd_block = lambda i, j: jnp.minimum(i * nj + j, n_wd - 1)
    return pl.pallas_call(
        _in_proj_kernel,
        grid=(m // tm, nj),
        in_specs=[_row_block_spec(tm, k),
                  _row_block_spec(tm, LANES),
                  pl.BlockSpec((None, tn, k), lambda i, j: (layer, j, 0)),
                  pl.BlockSpec((None, GATE_PAD, k), lambda i, j: (layer, MAIN_WIDTH // GATE_PAD, 0)),
                  pl.BlockSpec((None, tr, d), lambda i, j: (layer, wd_block(i, j), 0))],
        out_specs=[pl.BlockSpec((tm, tn), lambda i, j: (i, j)),
                   pl.BlockSpec((tm, GATE_PAD), lambda i, j: (i, 0)),
                   pl.BlockSpec((tr, d), lambda i, j: (wd_block(i, j), 0))],
        out_shape=[jax.ShapeDtypeStruct((m, MAIN_WIDTH), BF16),
                   jax.ShapeDtypeStruct((m, GATE_PAD), F32),
                   jax.ShapeDtypeStruct((f, d), BF16)],
        compiler_params=_params("arbitrary", "arbitrary"),
        name="in_proj",
    )(xg, r, w_in_t, w_in_t, w_down)


def _ffn_up_kernel(xg_ref, r_ref, wg_ref, wu_ref, o_ref, *, row_splits):
    wg = wg_ref[...].astype(BF16)
    wu = wu_ref[...].astype(BF16)
    rows = xg_ref.shape[0] // row_splits
    for s in range(row_splits):
        rs = slice(s * rows, (s + 1) * rows)
        g = _scale_rows(jnp.dot(xg_ref[rs, :], wg, preferred_element_type=F32), r_ref[rs, :])
        u = _scale_rows(jnp.dot(xg_ref[rs, :], wu, preferred_element_type=F32), r_ref[rs, :])
        o_ref[rs, :] = (_silu(g) * u).astype(o_ref.dtype)


def ffn_up(xg, r, w_gate, w_up, layer, tm=4096, tn=256, row_splits=4):
    m, k = xg.shape
    n = w_gate.shape[2]
    w_spec = pl.BlockSpec((None, k, tn), lambda i, j: (layer, 0, j))
    return pl.pallas_call(
        functools.partial(_ffn_up_kernel, row_splits=row_splits),
        grid=(m // tm, n // tn),
        in_specs=[_row_block_spec(tm, k), _row_block_spec(tm, LANES), w_spec, w_spec],
        out_specs=pl.BlockSpec((tm, tn), lambda i, j: (i, j)),
        out_shape=jax.ShapeDtypeStruct((m, n), BF16),
        compiler_params=_params("arbitrary", "arbitrary"),
        name="ffn_up",
    )(xg, r, w_gate, w_up)


def _rope_kernel(pos_ref, inv_ref, cos_ref, sin_ref):
    ang = pos_ref[...].astype(F32) * inv_ref[...]
    cos_ref[...] = jnp.cos(ang)
    sin_ref[...] = jnp.sin(ang)


def rope_tables(positions, tm=1024):
    s = positions.shape[0]
    tm = min(tm, s)
    half = RET_D // 2
    inv_freq = (ROPE_BASE ** (-np.arange(half, dtype=np.float32) / half)).astype(np.float32)
    return pl.pallas_call(
        _rope_kernel,
        grid=(s // tm,),
        in_specs=[pl.BlockSpec((tm, 1), lambda i: (i, 0)),
                  pl.BlockSpec((1, half), lambda i: (0, 0))],
        out_specs=[pl.BlockSpec((tm, half), lambda i: (i, 0))] * 2,
        out_shape=[jax.ShapeDtypeStruct((s, half), F32)] * 2,
        compiler_params=_params("arbitrary"),
        name="rope_tables",
    )(positions.reshape(s, 1), jnp.asarray(inv_freq).reshape(1, half))


def _ret_kernel(lg_ref, q_ref, k_ref, v_ref, g_ref, cos_ref, sin_ref, gain_ref, o_ref,
                state, decay, qdec, kdec):
    c = RET_CHUNK
    lg = lg_ref[pl.program_id(0)]

    @pl.when(pl.program_id(1) == 0)
    def _():
        state[...] = jnp.zeros_like(state)
        i = lax.broadcasted_iota(jnp.int32, (c, c), 0)
        j = lax.broadcasted_iota(jnp.int32, (c, c), 1)
        diff = (i - j).astype(F32)
        decay[...] = jnp.where(diff >= 0, jnp.exp(lg * jnp.maximum(diff, 0.0)), 0.0)
        r = lax.broadcasted_iota(jnp.int32, (c, RET_D), 0).astype(F32)
        qdec[...] = jnp.exp(lg * (r + 1.0))
        kdec[...] = jnp.exp(lg * (c - 1.0 - r))

    half = RET_D // 2
    cdec = jnp.exp(jnp.zeros((1, RET_D), F32) + lg * c)
    st = state[...]
    for ci in range(RET_CHUNKS_PER_STEP):
        rs = slice(ci * c, (ci + 1) * c)
        cos = cos_ref[rs, :]
        sin = sin_ref[rs, :]

        def rot(t):
            t1, t2 = t[:, :half], t[:, half:]
            return jnp.concatenate([t1 * cos - t2 * sin, t1 * sin + t2 * cos], axis=1)

        q = rot(q_ref[rs, :].astype(F32))
        k = rot(k_ref[rs, :].astype(F32)) * (RET_D ** -0.5)
        vb = v_ref[rs, :].astype(BF16)
        qb = q.astype(BF16)
        scores = lax.dot_general(qb, k.astype(BF16), NT_DIMS, preferred_element_type=F32) * decay[...]
        o = (jnp.dot(scores.astype(BF16), vb, preferred_element_type=F32)
             + jnp.dot(qb, st.astype(BF16), preferred_element_type=F32) * qdec[...])
        kd = (k * kdec[...]).astype(BF16)
        st = st * cdec + lax.dot_general(kd, vb, TN_DIMS, preferred_element_type=F32)

        mu = jnp.mean(o, axis=-1, keepdims=True)
        oc = o - mu
        var = jnp.mean(oc * oc, axis=-1, keepdims=True)
        y = oc * lax.rsqrt(var + EPS) * gain_ref[...]
        o_ref[rs, :] = (_silu(g_ref[rs, :].astype(F32)) * y).astype(o_ref.dtype)
    state[...] = st


def retention_heads(proj, cos, sin, gain):
    s = proj.shape[0]
    c = RET_CHUNK
    log_gamma = np.log1p(-np.exp2(-5.0 - np.arange(RET_HEADS, dtype=np.float32))).astype(np.float32)
    col = lambda base: (lambda h, n, lg: (n, base + h))
    rows = min(c * RET_CHUNKS_PER_STEP, s)
    grid_spec = pltpu.PrefetchScalarGridSpec(
        num_scalar_prefetch=1,
        grid=(RET_HEADS, s // rows),
        in_specs=[pl.BlockSpec((rows, RET_D), col(0)),
                  pl.BlockSpec((rows, RET_D), col(RET_HEADS)),
                  pl.BlockSpec((rows, RET_D), col(2 * RET_HEADS)),
                  pl.BlockSpec((rows, RET_D), col(3 * RET_HEADS)),
                  pl.BlockSpec((rows, RET_D // 2), lambda h, n, lg: (n, 0)),
                  pl.BlockSpec((rows, RET_D // 2), lambda h, n, lg: (n, 0)),
                  pl.BlockSpec((1, RET_D), lambda h, n, lg: (0, h))],
        out_specs=pl.BlockSpec((rows, RET_D), lambda h, n, lg: (n, h)),
        scratch_shapes=[pltpu.VMEM((RET_D, RET_D), F32),
                        pltpu.VMEM((c, c), F32),
                        pltpu.VMEM((c, RET_D), F32),
                        pltpu.VMEM((c, RET_D), F32)],
    )
    return pl.pallas_call(
        _ret_kernel,
        grid_spec=grid_spec,
        out_shape=jax.ShapeDtypeStruct((s, RET_WIDTH + GLA_WIDTH), BF16),
        compiler_params=_params("arbitrary", "arbitrary"),
        name="retention",
    )(jnp.asarray(log_gamma), proj, proj, proj, proj, cos, sin, gain.reshape(1, RET_WIDTH))


def _split3(x):
    hi = x.astype(BF16)
    r1 = x - hi.astype(F32)
    mid = r1.astype(BF16)
    lo = (r1 - mid.astype(F32)).astype(BF16)
    return hi, mid, lo


def _gla_kernel(q_ref, k_ref, v_ref, g_ref, a_ref, wup_ref, b_ref, gain_ref, mixed_ref, o_ref,
                state_t, pair_level, tri, attn_s):
    del mixed_ref
    c = GLA_CHUNK
    log2e = 1.4426950408889634

    @pl.when(pl.program_id(1) == 0)
    def _():
        state_t[...] = jnp.zeros_like(state_t)
        i = lax.broadcasted_iota(jnp.int32, (c, c), 0)
        j = lax.broadcasted_iota(jnp.int32, (c, c), 1)
        tri[...] = jnp.where(i >= j, 1.0, 0.0).astype(BF16)
        x = jnp.where(i > j, jnp.bitwise_xor(i, j), 0)
        lvl = jnp.full((c, c), GLA_LEVELS[-1] - 1, jnp.int32)
        for p in GLA_LEVELS:
            lvl = lvl + jnp.where(x >= (1 << p), 1, 0)
        pair_level[...] = lvl

    a_hi, a_mid, a_lo = _split3(a_ref[...])
    w_hi, w_mid, w_lo = _split3(wup_ref[...])
    dotf = lambda x, y: jnp.dot(x, y, preferred_element_type=F32)
    z = (dotf(a_hi, w_hi) + (dotf(a_hi, w_mid) + dotf(a_mid, w_hi))
         + (dotf(a_hi, w_lo) + dotf(a_lo, w_hi) + dotf(a_mid, w_mid))) + b_ref[...]
    la = (jnp.minimum(z, 0.0) - jnp.log1p(jnp.exp(-jnp.abs(z)))) * (log2e / GLA_GATE_TAU)
    l_hi, l_mid, l_lo = _split3(la)
    t = tri[...]
    cum = dotf(t, l_hi) + dotf(t, l_mid) + dotf(t, l_lo)

    qs = q_ref[...].astype(F32) * (GLA_DK ** -0.5)
    k = k_ref[...].astype(F32)
    vb = v_ref[...].astype(BF16)
    st = state_t[...]

    o = lax.dot_general((qs * jnp.exp2(cum)).astype(BF16), st.astype(BF16), NT_DIMS,
                        preferred_element_type=F32)

    attn_s[...] = jnp.zeros_like(attn_s)
    for p in GLA_LEVELS:
        half = 1 << p
        nb = c // (2 * half)
        q_rows, k_rows = [], []
        for blk in range(nb):
            up = slice(blk * 2 * half, blk * 2 * half + half)
            lo = slice(blk * 2 * half + half, (blk + 1) * 2 * half)
            bnd = cum[up.stop - 1:up.stop, :]
            e_lo = jnp.exp2(cum[lo] - bnd)
            k_rows += [k[up] * jnp.exp2(bnd - cum[up]), k[lo] * e_lo]
            q_rows.append(qs[lo] * e_lo)
        q_t = q_rows[0] if nb == 1 else jnp.concatenate(q_rows, axis=0)
        k_t = jnp.concatenate(k_rows, axis=0)
        part = lax.dot_general(q_t.astype(BF16), k_t.astype(BF16), NT_DIMS,
                               preferred_element_type=F32)
        for blk in range(nb):
            lo = slice(blk * 2 * half + half, (blk + 1) * 2 * half)
            attn_s[lo, :] = jnp.where(pair_level[lo, :] == p,
                                      part[blk * half:(blk + 1) * half, :], attn_s[lo, :])

    rows = lax.broadcasted_iota(jnp.int32, (SUBLANES, 1), 0)
    lane = lax.broadcasted_iota(jnp.int32, (SUBLANES, LANES), 1)
    for blk in range(c // SUBLANES):
        r = blk * SUBLANES
        rs = slice(r, r + SUBLANES)
        ls = slice(r // LANES * LANES, r // LANES * LANES + LANES)
        q8, k8, c8 = qs[rs], k[rs], cum[rs]
        tile = attn_s[rs, ls]
        for m in range(SUBLANES):
            w = jnp.exp2(c8 - c8[m:m + 1, :])
            a = jnp.sum(q8 * w * k8[m:m + 1, :], axis=-1, keepdims=True)
            a = jnp.where(rows >= m, a, 0.0)
            tile = jnp.where(lane == r % LANES + m, a, tile)
        attn_s[rs, ls] = tile
    o = o + jnp.dot(attn_s[...].astype(BF16), vb, preferred_element_type=F32)

    last = cum[c - 1:c, :]
    kd = (k * jnp.exp2(last - cum)).astype(BF16)
    state_t[...] = st * jnp.exp2(last) + lax.dot_general(vb, kd, TN_DIMS, preferred_element_type=F32)

    ms = jnp.mean(o * o, axis=-1, keepdims=True)
    y = o * lax.rsqrt(ms + EPS) * gain_ref[...]
    o_ref[...] = (_silu(g_ref[...].astype(F32)) * y).astype(o_ref.dtype)


def gla_heads(proj, ga, w_up, b, gain, mixed):
    s = proj.shape[0]
    c = GLA_CHUNK
    qk_base = 4 * RET_WIDTH // GLA_DK
    v_base = (4 * RET_WIDTH + 2 * GLA_KEY_WIDTH) // GLA_DV
    w_up_pad = jnp.zeros((GATE_PAD, GLA_KEY_WIDTH), F32).at[:GLA_GATE_RANK].set(w_up)
    return pl.pallas_call(
        _gla_kernel,
        grid=(GLA_HEADS, s // c),
        in_specs=[pl.BlockSpec((c, GLA_DK), lambda h, n: (n, qk_base + h)),
                  pl.BlockSpec((c, GLA_DK), lambda h, n: (n, qk_base + GLA_HEADS + h)),
                  pl.BlockSpec((c, GLA_DV), lambda h, n: (n, v_base + h)),
                  pl.BlockSpec((c, GLA_DV), lambda h, n: (n, v_base + GLA_HEADS + h)),
                  pl.BlockSpec((c, GATE_PAD), lambda h, n: (n, 0)),
                  pl.BlockSpec((GATE_PAD, GLA_DK), lambda h, n: (0, h)),
                  pl.BlockSpec((1, GLA_DK), lambda h, n: (0, h)),
                  pl.BlockSpec((1, GLA_DV), lambda h, n: (0, h)),
                  pl.BlockSpec(memory_space=pl.ANY)],
        out_specs=pl.BlockSpec((c, GLA_DV), lambda h, n: (n, RET_WIDTH // GLA_DV + h)),
        out_shape=jax.ShapeDtypeStruct(mixed.shape, mixed.dtype),
        input_output_aliases={8: 0},
        scratch_shapes=[pltpu.VMEM((GLA_DV, GLA_DK), F32),
                        pltpu.VMEM((c, c), jnp.int32),
                        pltpu.VMEM((c, c), BF16),
                        pltpu.VMEM((c, c), F32)],
        compiler_params=_params("arbitrary", "arbitrary"),
        name="gla",
    )(proj, proj, proj, proj, ga, w_up_pad, b.reshape(1, GLA_KEY_WIDTH), gain.reshape(1, GLA_WIDTH),
      mixed)


def kernel(x, positions, mix_norm, w_in, gla_w_up, gla_b, ret_gain, gla_gain, w_out, ffn_norm,
           w_gate, w_up, w_down, final_norm):
    b, s, d = x.shape
    x = x.reshape(b * s, d)
    cos, sin = rope_tables(positions.reshape(b * s))
    w_in_t = jnp.swapaxes(w_in, 1, 2)
    xg, r = norm_inputs(x, mix_norm[0])
    for l in range(DEPTH):
        proj, ga, wd = in_proj(xg, r, w_in_t, w_down, l)
        mixed = retention_heads(proj, cos, sin, ret_gain[l])
        mixed = gla_heads(proj, ga, gla_w_up[l], gla_b[l], gla_gain[l], mixed)
        x, xg, r = matmul_resid(mixed, w_out, x, layer=l, next_gain=ffn_norm[l], name="out_proj")
        hid = ffn_up(xg, r, w_gate, w_up, l)
        if l + 1 < DEPTH:
            x, xg, r = matmul_resid(hid, wd, x, next_gain=mix_norm[l + 1], tm=512, name="ffn_down")
        else:
            x = matmul_resid(hid, wd, x, tm=512, name="ffn_down")
    return rmsnorm(x, final_norm, F32).reshape(b, s, d)
```

```python
import functools

import numpy as np
import jax
import jax.numpy as jnp
from jax import lax
from jax.experimental import pallas as pl
from jax.experimental.pallas import tpu as pltpu

F32 = jnp.float32
BF16 = jnp.bfloat16

D_MODEL = 4096
DEPTH = 2
RET_HEADS = 8
RET_D = 256
RET_WIDTH = RET_HEADS * RET_D
GLA_HEADS = 4
GLA_DK = 256
GLA_DV = 512
GLA_KEY_WIDTH = GLA_HEADS * GLA_DK
GLA_WIDTH = GLA_HEADS * GLA_DV
GLA_GATE_RANK = 16
GLA_GATE_TAU = 16.0
FFN_HIDDEN = 11008
ROPE_BASE = 10000.0
EPS = 1e-6

MAIN_WIDTH = 4 * RET_WIDTH + 2 * GLA_KEY_WIDTH + 2 * GLA_WIDTH
LANES = 128
SUBLANES = 8
GATE_PAD = LANES

RET_CHUNK = 256
RET_CHUNKS_PER_STEP = 4
GLA_CHUNK = 256
GLA_LEVELS = (7, 6, 5, 4, 3)

VMEM_LIMIT = 62 * 1024 * 1024

NT_DIMS = (((1,), (1,)), ((), ()))
TN_DIMS = (((0,), (0,)), ((), ()))


def _params(*sem):
    return pltpu.CompilerParams(dimension_semantics=sem, vmem_limit_bytes=VMEM_LIMIT)


def _silu(x):
    return x * (1.0 / (1.0 + jnp.exp(-x)))


def _rmsnorm_kernel(x_ref, g_ref, o_ref):
    x = x_ref[...]
    ms = jnp.mean(x * x, axis=-1, keepdims=True)
    o_ref[...] = (x * lax.rsqrt(ms + EPS) * g_ref[...]).astype(o_ref.dtype)


def rmsnorm(x, gain, out_dtype, tm=256):
    s, d = x.shape
    return pl.pallas_call(
        _rmsnorm_kernel,
        grid=(s // tm,),
        in_specs=[pl.BlockSpec((tm, d), lambda i: (i, 0)),
                  pl.BlockSpec((1, d), lambda i: (0, 0))],
        out_specs=pl.BlockSpec((tm, d), lambda i: (i, 0)),
        out_shape=jax.ShapeDtypeStruct((s, d), out_dtype),
        compiler_params=_params("arbitrary"),
        name="rmsnorm",
    )(x, gain.reshape(1, d))


def _fold_lanes(v):
    acc = v[:, :LANES]
    for g in range(1, v.shape[1] // LANES):
        acc = acc + v[:, g * LANES:(g + 1) * LANES]
    return acc


def _row_scale(ss):
    r = lax.rsqrt(jnp.sum(ss, axis=-1, keepdims=True) * (1.0 / D_MODEL) + EPS)
    return jnp.broadcast_to(r, ss.shape)


def _scale_rows(acc, r):
    return acc * jnp.concatenate([r] * (acc.shape[1] // LANES), axis=1)


def _norm_inputs_kernel(x_ref, g_ref, xg_ref, r_ref):
    x = x_ref[...]
    xg_ref[...] = (x * g_ref[...]).astype(BF16)
    r_ref[...] = _row_scale(_fold_lanes(x * x))


def norm_inputs(x, gain, tm=256):
    s, d = x.shape
    return pl.pallas_call(
        _norm_inputs_kernel,
        grid=(s // tm,),
        in_specs=[pl.BlockSpec((tm, d), lambda i: (i, 0)),
                  pl.BlockSpec((1, d), lambda i: (0, 0))],
        out_specs=[pl.BlockSpec((tm, d), lambda i: (i, 0)),
                   pl.BlockSpec((tm, LANES), lambda i: (i, 0))],
        out_shape=[jax.ShapeDtypeStruct((s, d), BF16),
                   jax.ShapeDtypeStruct((s, LANES), F32)],
        compiler_params=_params("arbitrary"),
        name="norm_inputs",
    )(x, gain.reshape(1, d))


def _row_block_spec(tm, k, single_buffer=True):
    return pl.BlockSpec((tm, k), lambda i, j: (i, 0),
                        pipeline_mode=pl.Buffered(1) if single_buffer else None)


def _mm_resid_kernel(*all_refs, n_parts):
    operand_refs, res_ref, refs = all_refs[:2 * n_parts], all_refs[2 * n_parts], all_refs[2 * n_parts + 1:]
    o_ref = refs[-1] if len(refs) == 1 else refs[1]
    x = res_ref[...]
    for p in range(n_parts):
        a_ref, w_ref = operand_refs[2 * p], operand_refs[2 * p + 1]
        x = x + jnp.dot(a_ref[...], w_ref[...].astype(BF16), preferred_element_type=F32)
    o_ref[...] = x
    if len(refs) > 1:
        g_ref, _, xg_ref, r_ref = refs
        xg_ref[...] = (x * g_ref[...]).astype(BF16)
        part = _fold_lanes(x * x)
        j = pl.program_id(1)

        @pl.when(j == 0)
        def _():
            r_ref[...] = part

        @pl.when(j > 0)
        def _():
            r_ref[...] += part

        @pl.when(j == pl.num_programs(1) - 1)
        def _():
            r_ref[...] = _row_scale(r_ref[...])


def matmul_resid(a_parts, w, resid, layer=None, next_gain=None, tm=1024, tn=512, name="matmul"):
    m, k = a_parts[0].shape
    n = w.shape[-1]
    tile = pl.BlockSpec((tm, tn), lambda i, j: (i, j))
    in_specs, args = [], []
    for p, a in enumerate(a_parts):
        in_specs.append(_row_block_spec(tm, k, False))
        if w.ndim == 3:
            in_specs.append(pl.BlockSpec((None, k, tn), lambda i, j, p=p: (layer, p, j)))
        else:
            in_specs.append(pl.BlockSpec((k, tn), lambda i, j, p=p: (p, j)))
        args += [a, w]
    in_specs.append(tile)
    args.append(resid)
    out_specs, out_shape = [tile], [jax.ShapeDtypeStruct((m, n), F32)]
    if next_gain is not None:
        in_specs.append(pl.BlockSpec((1, tn), lambda i, j: (0, j)))
        args.append(next_gain.reshape(1, n))
        out_specs += [tile, pl.BlockSpec((tm, LANES), lambda i, j: (i, 0))]
        out_shape += [jax.ShapeDtypeStruct((m, n), BF16), jax.ShapeDtypeStruct((m, LANES), F32)]
    outs = pl.pallas_call(
        functools.partial(_mm_resid_kernel, n_parts=len(a_parts)),
        grid=(m // tm, n // tn),
        in_specs=in_specs,
        out_specs=out_specs,
        out_shape=out_shape,
        compiler_params=_params("arbitrary", "arbitrary"),
        name=name,
    )(*args)
    return outs if next_gain is not None else outs[0]


def _in_proj_kernel(xg_ref, r_ref, w_ref, wlr_ref, wd_ref, o_ref, ga_ref, wd_bf16_ref):
    acc = lax.dot_general(xg_ref[...], w_ref[...].astype(BF16), NT_DIMS, preferred_element_type=F32)
    o_ref[...] = _scale_rows(acc, r_ref[...]).astype(o_ref.dtype)
    wd_bf16_ref[...] = wd_ref[...].astype(BF16)

    @pl.when(pl.program_id(1) == 0)
    def _():
        wlr = wlr_ref[...]
        row = lax.broadcasted_iota(jnp.int32, wlr.shape, 0)
        wlr = jnp.where(row < GLA_GATE_RANK, wlr, 0.0)
        ga = lax.dot_general(xg_ref[...], wlr.astype(BF16), NT_DIMS, preferred_element_type=F32)
        ga_ref[...] = ga * r_ref[...]


def in_proj(xg, r, w_in_t, w_down, layer, tm=2048, tn=512, tr=128):
    m, k = xg.shape
    _, f, d = w_down.shape
    nj = MAIN_WIDTH // tn
    n_wd = f // tr
    assert n_wd * tr == f and n_wd <= (m // tm) * nj
    wd_block = lambda i, j: jnp.minimum(i * nj + j, n_wd - 1)
    return pl.pallas_call(
        _in_proj_kernel,
        grid=(m // tm, nj),
        in_specs=[_row_block_spec(tm, k),
                  _row_block_spec(tm, LANES),
                  pl.BlockSpec((None, tn, k), lambda i, j: (layer, j, 0)),
                  pl.BlockSpec((None, GATE_PAD, k), lambda i, j: (layer, MAIN_WIDTH // GATE_PAD, 0)),
                  pl.BlockSpec((None, tr, d), lambda i, j: (layer, wd_block(i, j), 0))],
        out_specs=[pl.BlockSpec((tm, tn), lambda i, j: (i, j)),
                   pl.BlockSpec((tm, GATE_PAD), lambda i, j: (i, 0)),
                   pl.BlockSpec((tr, d), lambda i, j: (wd_block(i, j), 0))],
        out_shape=[jax.ShapeDtypeStruct((m, MAIN_WIDTH), BF16),
                   jax.ShapeDtypeStruct((m, GATE_PAD), F32),
                   jax.ShapeDtypeStruct((f, d), BF16)],
        compiler_params=_params("arbitrary", "arbitrary"),
        name="in_proj",
    )(xg, r, w_in_t, w_in_t, w_down)


def _ffn_up_kernel(xg_ref, r_ref, wg_ref, wu_ref, o_ref, *, row_splits):
    wg = wg_ref[...].astype(BF16)
    wu = wu_ref[...].astype(BF16)
    rows = xg_ref.shape[0] // row_splits
    for s in range(row_splits):
        rs = slice(s * rows, (s + 1) * rows)
        g = _scale_rows(jnp.dot(xg_ref[rs, :], wg, preferred_element_type=F32), r_ref[rs, :])
        u = _scale_rows(jnp.dot(xg_ref[rs, :], wu, preferred_element_type=F32), r_ref[rs, :])
        o_ref[rs, :] = (_silu(g) * u).astype(o_ref.dtype)


def ffn_up(xg, r, w_gate, w_up, layer, tm=4096, tn=256, row_splits=4):
    m, k = xg.shape
    n = w_gate.shape[2]
    w_spec = pl.BlockSpec((None, k, tn), lambda i, j: (layer, 0, j))
    return pl.pallas_call(
        functools.partial(_ffn_up_kernel, row_splits=row_splits),
        grid=(m // tm, n // tn),
        in_specs=[_row_block_spec(tm, k), _row_block_spec(tm, LANES), w_spec, w_spec],
        out_specs=pl.BlockSpec((tm, tn), lambda i, j: (i, j)),
        out_shape=jax.ShapeDtypeStruct((m, n), BF16),
        compiler_params=_params("arbitrary", "arbitrary"),
        name="ffn_up",
    )(xg, r, w_gate, w_up)


def _rope_kernel(pos_ref, inv_ref, cos_ref, sin_ref):
    ang = pos_ref[...].astype(F32) * inv_ref[...]
    cos_ref[...] = jnp.cos(ang)
    sin_ref[...] = jnp.sin(ang)


def rope_tables(positions, tm=1024):
    s = positions.shape[0]
    tm = min(tm, s)
    half = RET_D // 2
    inv_freq = (ROPE_BASE ** (-np.arange(half, dtype=np.float32) / half)).astype(np.float32)
    return pl.pallas_call(
        _rope_kernel,
        grid=(s // tm,),
        in_specs=[pl.BlockSpec((tm, 1), lambda i: (i, 0)),
                  pl.BlockSpec((1, half), lambda i: (0, 0))],
        out_specs=[pl.BlockSpec((tm, half), lambda i: (i, 0))] * 2,
        out_shape=[jax.ShapeDtypeStruct((s, half), F32)] * 2,
        compiler_params=_params("arbitrary"),
        name="rope_tables",
    )(positions.reshape(s, 1), jnp.asarray(inv_freq).reshape(1, half))


def _ret_kernel(lg_ref, q_ref, k_ref, v_ref, g_ref, cos_ref, sin_ref, gain_ref, o_ref,
                state, decay, qdec, kdec):
    c = RET_CHUNK
    lg = lg_ref[pl.program_id(0)]

    @pl.when(pl.program_id(1) == 0)
    def _():
        state[...] = jnp.zeros_like(state)
        i = lax.broadcasted_iota(jnp.int32, (c, c), 0)
        j = lax.broadcasted_iota(jnp.int32, (c, c), 1)
        diff = (i - j).astype(F32)
        decay[...] = jnp.where(diff >= 0, jnp.exp(lg * jnp.maximum(diff, 0.0)), 0.0)
        r = lax.broadcasted_iota(jnp.int32, (c, RET_D), 0).astype(F32)
        qdec[...] = jnp.exp(lg * (r + 1.0))
        kdec[...] = jnp.exp(lg * (c - 1.0 - r))

    half = RET_D // 2
    cdec = jnp.exp(jnp.zeros((1, RET_D), F32) + lg * c)
    st = state[...]
    for ci in range(RET_CHUNKS_PER_STEP):
        rs = slice(ci * c, (ci + 1) * c)
        cos = cos_ref[rs, :]
        sin = sin_ref[rs, :]

        def rot(t):
            t1, t2 = t[:, :half], t[:, half:]
            return jnp.concatenate([t1 * cos - t2 * sin, t1 * sin + t2 * cos], axis=1)

        q = rot(q_ref[rs, :].astype(F32))
        k = rot(k_ref[rs, :].astype(F32)) * (RET_D ** -0.5)
        vb = v_ref[rs, :].astype(BF16)
        qb = q.astype(BF16)
        scores = lax.dot_general(qb, k.astype(BF16), NT_DIMS, preferred_element_type=F32) * decay[...]
        o = (jnp.dot(scores.astype(BF16), vb, preferred_element_type=F32)
             + jnp.dot(qb, st.astype(BF16), preferred_element_type=F32) * qdec[...])
        kd = (k * kdec[...]).astype(BF16)
        st = st * cdec + lax.dot_general(kd, vb, TN_DIMS, preferred_element_type=F32)

        mu = jnp.mean(o, axis=-1, keepdims=True)
        oc = o - mu
        var = jnp.mean(oc * oc, axis=-1, keepdims=True)
        y = oc * lax.rsqrt(var + EPS) * gain_ref[...]
        o_ref[rs, :] = (_silu(g_ref[rs, :].astype(F32)) * y).astype(o_ref.dtype)
    state[...] = st


def retention_heads(proj, cos, sin, gain):
    s = proj.shape[0]
    c = RET_CHUNK
    log_gamma = np.log1p(-np.exp2(-5.0 - np.arange(RET_HEADS, dtype=np.float32))).astype(np.float32)
    col = lambda base: (lambda h, n, lg: (n, base + h))
    rows = min(c * RET_CHUNKS_PER_STEP, s)
    grid_spec = pltpu.PrefetchScalarGridSpec(
        num_scalar_prefetch=1,
        grid=(RET_HEADS, s // rows),
        in_specs=[pl.BlockSpec((rows, RET_D), col(0)),
                  pl.BlockSpec((rows, RET_D), col(RET_HEADS)),
                  pl.BlockSpec((rows, RET_D), col(2 * RET_HEADS)),
                  pl.BlockSpec((rows, RET_D), col(3 * RET_HEADS)),
                  pl.BlockSpec((rows, RET_D // 2), lambda h, n, lg: (n, 0)),
                  pl.BlockSpec((rows, RET_D // 2), lambda h, n, lg: (n, 0)),
                  pl.BlockSpec((1, RET_D), lambda h, n, lg: (0, h))],
        out_specs=pl.BlockSpec((rows, RET_D), lambda h, n, lg: (n, h)),
        scratch_shapes=[pltpu.VMEM((RET_D, RET_D), F32),
                        pltpu.VMEM((c, c), F32),
                        pltpu.VMEM((c, RET_D), F32),
                        pltpu.VMEM((c, RET_D), F32)],
    )
    return pl.pallas_call(
        _ret_kernel,
        grid_spec=grid_spec,
        out_shape=jax.ShapeDtypeStruct((s, RET_WIDTH), BF16),
        compiler_params=_params("arbitrary", "arbitrary"),
        name="retention",
    )(jnp.asarray(log_gamma), proj, proj, proj, proj, cos, sin, gain.reshape(1, RET_WIDTH))


def _split3(x):
    hi = x.astype(BF16)
    r1 = x - hi.astype(F32)
    mid = r1.astype(BF16)
    lo = (r1 - mid.astype(F32)).astype(BF16)
    return hi, mid, lo


def _gla_kernel(q_ref, k_ref, v_ref, g_ref, a_ref, wup_ref, b_ref, gain_ref, o_ref,
                state_t, pair_level, tri, attn_s):
    c = GLA_CHUNK
    log2e = 1.4426950408889634

    @pl.when(pl.program_id(1) == 0)
    def _():
        state_t[...] = jnp.zeros_like(state_t)
        i = lax.broadcasted_iota(jnp.int32, (c, c), 0)
        j = lax.broadcasted_iota(jnp.int32, (c, c), 1)
        tri[...] = jnp.where(i >= j, 1.0, 0.0).astype(BF16)
        x = jnp.where(i > j, jnp.bitwise_xor(i, j), 0)
        lvl = jnp.full((c, c), GLA_LEVELS[-1] - 1, jnp.int32)
        for p in GLA_LEVELS:
            lvl = lvl + jnp.where(x >= (1 << p), 1, 0)
        pair_level[...] = lvl

    a_hi, a_mid, a_lo = _split3(a_ref[...])
    w_hi, w_mid, w_lo = _split3(wup_ref[...])
    dotf = lambda x, y: jnp.dot(x, y, preferred_element_type=F32)
    z = (dotf(a_hi, w_hi) + (dotf(a_hi, w_mid) + dotf(a_mid, w_hi))
         + (dotf(a_hi, w_lo) + dotf(a_lo, w_hi) + dotf(a_mid, w_mid))) + b_ref[...]
    la = (jnp.minimum(z, 0.0) - jnp.log1p(jnp.exp(-jnp.abs(z)))) * (log2e / GLA_GATE_TAU)
    l_hi, l_mid, l_lo = _split3(la)
    t = tri[...]
    cum = dotf(t, l_hi) + dotf(t, l_mid) + dotf(t, l_lo)

    qs = q_ref[...].astype(F32) * (GLA_DK ** -0.5)
    k = k_ref[...].astype(F32)
    vb = v_ref[...].astype(BF16)
    st = state_t[...]

    o = lax.dot_general((qs * jnp.exp2(cum)).astype(BF16), st.astype(BF16), NT_DIMS,
                        preferred_element_type=F32)

    attn_s[...] = jnp.zeros_like(attn_s)
    for p in GLA_LEVELS:
        half = 1 << p
        nb = c // (2 * half)
        q_rows, k_rows = [], []
        for blk in range(nb):
            up = slice(blk * 2 * half, blk * 2 * half + half)
            lo = slice(blk * 2 * half + half, (blk + 1) * 2 * half)
            bnd = cum[up.stop - 1:up.stop, :]
            e_lo = jnp.exp2(cum[lo] - bnd)
            k_rows += [k[up] * jnp.exp2(bnd - cum[up]), k[lo] * e_lo]
            q_rows.append(qs[lo] * e_lo)
        q_t = q_rows[0] if nb == 1 else jnp.concatenate(q_rows, axis=0)
        k_t = jnp.concatenate(k_rows, axis=0)
        part = lax.dot_general(q_t.astype(BF16), k_t.astype(BF16), NT_DIMS,
                               preferred_element_type=F32)
        for blk in range(nb):
            lo = slice(blk * 2 * half + half, (blk + 1) * 2 * half)
            attn_s[lo, :] = jnp.where(pair_level[lo, :] == p,
                                      part[blk * half:(blk + 1) * half, :], attn_s[lo, :])

    rows = lax.broadcasted_iota(jnp.int32, (SUBLANES, 1), 0)
    lane = lax.broadcasted_iota(jnp.int32, (SUBLANES, LANES), 1)
    for blk in range(c // SUBLANES):
        r = blk * SUBLANES
        rs = slice(r, r + SUBLANES)
        ls = slice(r // LANES * LANES, r // LANES * LANES + LANES)
        q8, k8, c8 = qs[rs], k[rs], cum[rs]
        tile = attn_s[rs, ls]
        for m in range(SUBLANES):
            w = jnp.exp2(c8 - c8[m:m + 1, :])
            a = jnp.sum(q8 * w * k8[m:m + 1, :], axis=-1, keepdims=True)
            a = jnp.where(rows >= m, a, 0.0)
            tile = jnp.where(lane == r % LANES + m, a, tile)
        attn_s[rs, ls] = tile
    o = o + jnp.dot(attn_s[...].astype(BF16), vb, preferred_element_type=F32)

    last = cum[c - 1:c, :]
    kd = (k * jnp.exp2(last - cum)).astype(BF16)
    state_t[...] = st * jnp.exp2(last) + lax.dot_general(vb, kd, TN_DIMS, preferred_element_type=F32)

    ms = jnp.mean(o * o, axis=-1, keepdims=True)
    y = o * lax.rsqrt(ms + EPS) * gain_ref[...]
    o_ref[...] = (_silu(g_ref[...].astype(F32)) * y).astype(o_ref.dtype)


def gla_heads(proj, ga, w_up, b, gain):
    s = proj.shape[0]
    c = GLA_CHUNK
    qk_base = 4 * RET_WIDTH // GLA_DK
    v_base = (4 * RET_WIDTH + 2 * GLA_KEY_WIDTH) // GLA_DV
    w_up_pad = jnp.zeros((GATE_PAD, GLA_KEY_WIDTH), F32).at[:GLA_GATE_RANK].set(w_up)
    return pl.pallas_call(
        _gla_kernel,
        grid=(GLA_HEADS, s // c),
        in_specs=[pl.BlockSpec((c, GLA_DK), lambda h, n: (n, qk_base + h)),
                  pl.BlockSpec((c, GLA_DK), lambda h, n: (n, qk_base + GLA_HEADS + h)),
                  pl.BlockSpec((c, GLA_DV), lambda h, n: (n, v_base + h)),
                  pl.BlockSpec((c, GLA_DV), lambda h, n: (n, v_base + GLA_HEADS + h)),
                  pl.BlockSpec((c, GATE_PAD), lambda h, n: (n, 0)),
                  pl.BlockSpec((GATE_PAD, GLA_DK), lambda h, n: (0, h)),
                  pl.BlockSpec((1, GLA_DK), lambda h, n: (0, h)),
                  pl.BlockSpec((1, GLA_DV), lambda h, n: (0, h))],
        out_specs=pl.BlockSpec((c, GLA_DV), lambda h, n: (n, h)),
        out_shape=jax.ShapeDtypeStruct((s, GLA_WIDTH), BF16),
        scratch_shapes=[pltpu.VMEM((GLA_DV, GLA_DK), F32),
                        pltpu.VMEM((c, c), jnp.int32),
                        pltpu.VMEM((c, c), BF16),
                        pltpu.VMEM((c, c), F32)],
        compiler_params=_params("arbitrary", "arbitrary"),
        name="gla",
    )(proj, proj, proj, proj, ga, w_up_pad, b.reshape(1, GLA_KEY_WIDTH), gain.reshape(1, GLA_WIDTH))


def kernel(x, positions, mix_norm, w_in, gla_w_up, gla_b, ret_gain, gla_gain, w_out, ffn_norm,
           w_gate, w_up, w_down, final_norm):
    b, s, d = x.shape
    x = x.reshape(b * s, d)
    cos, sin = rope_tables(positions.reshape(b * s))
    w_in_t = jnp.swapaxes(w_in, 1, 2)
    xg, r = norm_inputs(x, mix_norm[0])
    for l in range(DEPTH):
        proj, ga, wd = in_proj(xg, r, w_in_t, w_down, l)
        r_out = retention_heads(proj, cos, sin, ret_gain[l])
        g_out = gla_heads(proj, ga, gla_w_up[l], gla_b[l], gla_gain[l])
        x, xg, r = matmul_resid((r_out, g_out), w_out, x, layer=l, next_gain=ffn_norm[l],
                                name="out_proj")
        hid = ffn_up(xg, r, w_gate, w_up, l)
        if l + 1 < DEPTH:
            x, xg, r = matmul_resid((hid,), wd, x, next_gain=mix_norm[l + 1], tm=512, name="ffn_down")
        else:
            x = matmul_resid((hid,), wd, x, tm=512, name="ffn_down")
    return rmsnorm(x, final_norm, F32).reshape(b, s, d)
```

```python
import functools

import numpy as np
import jax
import jax.numpy as jnp
from jax import lax
from jax.experimental import pallas as pl
from jax.experimental.pallas import tpu as pltpu

F32 = jnp.float32
BF16 = jnp.bfloat16

D_MODEL = 4096
DEPTH = 2
RET_HEADS = 8
RET_D = 256
RET_WIDTH = RET_HEADS * RET_D
GLA_HEADS = 4
GLA_DK = 256
GLA_DV = 512
GLA_KEY_WIDTH = GLA_HEADS * GLA_DK
GLA_WIDTH = GLA_HEADS * GLA_DV
GLA_GATE_RANK = 16
GLA_GATE_TAU = 16.0
FFN_HIDDEN = 11008
ROPE_BASE = 10000.0
EPS = 1e-6

MAIN_WIDTH = 4 * RET_WIDTH + 2 * GLA_KEY_WIDTH + 2 * GLA_WIDTH
LANES = 128
SUBLANES = 8
GATE_PAD = LANES

RET_CHUNK = 256
RET_CHUNKS_PER_STEP = 4
GLA_CHUNK = 256
GLA_CHUNKS_PER_STEP = 2
GLA_LEVELS = (7, 6, 5, 4, 3)

VMEM_LIMIT = 62 * 1024 * 1024

NT_DIMS = (((1,), (1,)), ((), ()))
TN_DIMS = (((0,), (0,)), ((), ()))


def _params(*sem):
    return pltpu.CompilerParams(dimension_semantics=sem, vmem_limit_bytes=VMEM_LIMIT)


def _silu(x):
    return x * (1.0 / (1.0 + jnp.exp(-x)))


def _rmsnorm_kernel(x_ref, g_ref, o_ref):
    x = x_ref[...]
    ms = jnp.mean(x * x, axis=-1, keepdims=True)
    o_ref[...] = (x * lax.rsqrt(ms + EPS) * g_ref[...]).astype(o_ref.dtype)


def rmsnorm(x, gain, out_dtype, tm=256):
    s, d = x.shape
    return pl.pallas_call(
        _rmsnorm_kernel,
        grid=(s // tm,),
        in_specs=[pl.BlockSpec((tm, d), lambda i: (i, 0)),
                  pl.BlockSpec((1, d), lambda i: (0, 0))],
        out_specs=pl.BlockSpec((tm, d), lambda i: (i, 0)),
        out_shape=jax.ShapeDtypeStruct((s, d), out_dtype),
        compiler_params=_params("arbitrary"),
        name="rmsnorm",
    )(x, gain.reshape(1, d))


def _fold_lanes(v):
    acc = v[:, :LANES]
    for g in range(1, v.shape[1] // LANES):
        acc = acc + v[:, g * LANES:(g + 1) * LANES]
    return acc


def _row_scale(ss):
    r = lax.rsqrt(jnp.sum(ss, axis=-1, keepdims=True) * (1.0 / D_MODEL) + EPS)
    return jnp.broadcast_to(r, ss.shape)


def _scale_rows(acc, r):
    return acc * jnp.concatenate([r] * (acc.shape[1] // LANES), axis=1)


def _norm_inputs_kernel(x_ref, g_ref, xg_ref, r_ref):
    x = x_ref[...]
    xg_ref[...] = (x * g_ref[...]).astype(BF16)
    r_ref[...] = _row_scale(_fold_lanes(x * x))


def norm_inputs(x, gain, tm=256):
    s, d = x.shape
    return pl.pallas_call(
        _norm_inputs_kernel,
        grid=(s // tm,),
        in_specs=[pl.BlockSpec((tm, d), lambda i: (i, 0)),
                  pl.BlockSpec((1, d), lambda i: (0, 0))],
        out_specs=[pl.BlockSpec((tm, d), lambda i: (i, 0)),
                   pl.BlockSpec((tm, LANES), lambda i: (i, 0))],
        out_shape=[jax.ShapeDtypeStruct((s, d), BF16),
                   jax.ShapeDtypeStruct((s, LANES), F32)],
        compiler_params=_params("arbitrary"),
        name="norm_inputs",
    )(x, gain.reshape(1, d))


def _row_block_spec(tm, k, single_buffer=True):
    return pl.BlockSpec((tm, k), lambda i, j: (i, 0),
                        pipeline_mode=pl.Buffered(1) if single_buffer else None)


def _mm_resid_kernel(*all_refs, n_parts, row_splits):
    operand_refs, res_ref, refs = all_refs[:2 * n_parts], all_refs[2 * n_parts], all_refs[2 * n_parts + 1:]
    emit_norm = len(refs) > 1
    if emit_norm:
        g_ref, o_ref, xg_ref, r_ref = refs
        j = pl.program_id(1)

        @pl.when(j == 0)
        def _():
            r_ref[...] = jnp.zeros_like(r_ref)
    else:
        o_ref, = refs

    ws = [operand_refs[2 * p + 1][...].astype(BF16) for p in range(n_parts)]
    rows = o_ref.shape[0] // row_splits
    for s in range(row_splits):
        rs = slice(s * rows, (s + 1) * rows)
        x = res_ref[rs, :]
        for p in range(n_parts):
            x = x + jnp.dot(operand_refs[2 * p][rs, :], ws[p], preferred_element_type=F32)
        o_ref[rs, :] = x
        if emit_norm:
            xg_ref[rs, :] = (x * g_ref[...]).astype(BF16)
            r_ref[rs, :] += _fold_lanes(x * x)

    if emit_norm:
        @pl.when(j == pl.num_programs(1) - 1)
        def _():
            r_ref[...] = _row_scale(r_ref[...])


def matmul_resid(a_parts, w, resid, layer=None, next_gain=None, tm=1024, tn=512,
                 single_buffer_a=False, row_splits=1, name="matmul"):
    m, k = a_parts[0].shape
    n = w.shape[-1]
    tile = pl.BlockSpec((tm, tn), lambda i, j: (i, j))
    in_specs, args = [], []
    for p, a in enumerate(a_parts):
        in_specs.append(_row_block_spec(tm, k, single_buffer_a))
        if w.ndim == 3:
            in_specs.append(pl.BlockSpec((None, k, tn), lambda i, j, p=p: (layer, p, j)))
        else:
            in_specs.append(pl.BlockSpec((k, tn), lambda i, j, p=p: (p, j)))
        args += [a, w]
    in_specs.append(tile)
    args.append(resid)
    out_specs, out_shape = [tile], [jax.ShapeDtypeStruct((m, n), F32)]
    if next_gain is not None:
        in_specs.append(pl.BlockSpec((1, tn), lambda i, j: (0, j)))
        args.append(next_gain.reshape(1, n))
        out_specs += [tile, pl.BlockSpec((tm, LANES), lambda i, j: (i, 0))]
        out_shape += [jax.ShapeDtypeStruct((m, n), BF16), jax.ShapeDtypeStruct((m, LANES), F32)]
    outs = pl.pallas_call(
        functools.partial(_mm_resid_kernel, n_parts=len(a_parts), row_splits=row_splits),
        grid=(m // tm, n // tn),
        in_specs=in_specs,
        out_specs=out_specs,
        out_shape=out_shape,
        compiler_params=_params("arbitrary", "arbitrary"),
        name=name,
    )(*args)
    return outs if next_gain is not None else outs[0]


def _in_proj_kernel(xg_ref, r_ref, w_ref, wlr_ref, wd_ref, o_ref, ga_ref, wd_bf16_ref):
    acc = lax.dot_general(xg_ref[...], w_ref[...].astype(BF16), NT_DIMS, preferred_element_type=F32)
    o_ref[...] = _scale_rows(acc, r_ref[...]).astype(o_ref.dtype)
    wd_bf16_ref[...] = wd_ref[...].astype(BF16)

    @pl.when(pl.program_id(1) == 0)
    def _():
        wlr = wlr_ref[...]
        row = lax.broadcasted_iota(jnp.int32, wlr.shape, 0)
        wlr = jnp.where(row < GLA_GATE_RANK, wlr, 0.0)
        ga = lax.dot_general(xg_ref[...], wlr.astype(BF16), NT_DIMS, preferred_element_type=F32)
        ga_ref[...] = ga * r_ref[...]


def in_proj(xg, r, w_in_t, w_down, layer, tm=2048, tn=512, tr=128):
    m, k = xg.shape
    _, f, d = w_down.shape
    nj = MAIN_WIDTH // tn
    n_wd = f // tr
    assert n_wd * tr == f and n_wd <= (m // tm) * nj
    wd_block = lambda i, j: jnp.minimum(i * nj + j, n_wd - 1)
    return pl.pallas_call(
        _in_proj_kernel,
        grid=(m // tm, nj),
        in_specs=[_row_block_spec(tm, k),
                  _row_block_spec(tm, LANES),
                  pl.BlockSpec((None, tn, k), lambda i, j: (layer, j, 0)),
                  pl.BlockSpec((None, GATE_PAD, k), lambda i, j: (layer, MAIN_WIDTH // GATE_PAD, 0)),
                  pl.BlockSpec((None, tr, d), lambda i, j: (layer, wd_block(i, j), 0))],
        out_specs=[pl.BlockSpec((tm, tn), lambda i, j: (i, j)),
                   pl.BlockSpec((tm, GATE_PAD), lambda i, j: (i, 0)),
                   pl.BlockSpec((tr, d), lambda i, j: (wd_block(i, j), 0))],
        out_shape=[jax.ShapeDtypeStruct((m, MAIN_WIDTH), BF16),
                   jax.ShapeDtypeStruct((m, GATE_PAD), F32),
                   jax.ShapeDtypeStruct((f, d), BF16)],
        compiler_params=_params("arbitrary", "arbitrary"),
        name="in_proj",
    )(xg, r, w_in_t, w_in_t, w_down)


def _ffn_up_kernel(xg_ref, r_ref, wg_ref, wu_ref, o_ref, *, row_splits):
    wg = wg_ref[...].astype(BF16)
    wu = wu_ref[...].astype(BF16)
    rows = xg_ref.shape[0] // row_splits
    for s in range(row_splits):
        rs = slice(s * rows, (s + 1) * rows)
        g = _scale_rows(jnp.dot(xg_ref[rs, :], wg, preferred_element_type=F32), r_ref[rs, :])
        u = _scale_rows(jnp.dot(xg_ref[rs, :], wu, preferred_element_type=F32), r_ref[rs, :])
        o_ref[rs, :] = (_silu(g) * u).astype(o_ref.dtype)


def ffn_up(xg, r, w_gate, w_up, layer, tm=4096, tn=256, row_splits=4):
    m, k = xg.shape
    n = w_gate.shape[2]
    w_spec = pl.BlockSpec((None, k, tn), lambda i, j: (layer, 0, j))
    return pl.pallas_call(
        functools.partial(_ffn_up_kernel, row_splits=row_splits),
        grid=(m // tm, n // tn),
        in_specs=[_row_block_spec(tm, k), _row_block_spec(tm, LANES), w_spec, w_spec],
        out_specs=pl.BlockSpec((tm, tn), lambda i, j: (i, j)),
        out_shape=jax.ShapeDtypeStruct((m, n), BF16),
        compiler_params=_params("arbitrary", "arbitrary"),
        name="ffn_up",
    )(xg, r, w_gate, w_up)


def _rope_kernel(pos_ref, inv_ref, cos_ref, sin_ref):
    ang = pos_ref[...].astype(F32) * inv_ref[...]
    cos_ref[...] = jnp.cos(ang)
    sin_ref[...] = jnp.sin(ang)


def rope_tables(positions, tm=1024):
    s = positions.shape[0]
    tm = min(tm, s)
    half = RET_D // 2
    inv_freq = (ROPE_BASE ** (-np.arange(half, dtype=np.float32) / half)).astype(np.float32)
    return pl.pallas_call(
        _rope_kernel,
        grid=(s // tm,),
        in_specs=[pl.BlockSpec((tm, 1), lambda i: (i, 0)),
                  pl.BlockSpec((1, half), lambda i: (0, 0))],
        out_specs=[pl.BlockSpec((tm, half), lambda i: (i, 0))] * 2,
        out_shape=[jax.ShapeDtypeStruct((s, half), F32)] * 2,
        compiler_params=_params("arbitrary"),
        name="rope_tables",
    )(positions.reshape(s, 1), jnp.asarray(inv_freq).reshape(1, half))


def _ret_kernel(lg_ref, q_ref, k_ref, v_ref, g_ref, cos_ref, sin_ref, gain_ref, o_ref,
                state, decay, qdec, kdec):
    c = RET_CHUNK
    lg = lg_ref[pl.program_id(0)]

    @pl.when(pl.program_id(1) == 0)
    def _():
        state[...] = jnp.zeros_like(state)
        i = lax.broadcasted_iota(jnp.int32, (c, c), 0)
        j = lax.broadcasted_iota(jnp.int32, (c, c), 1)
        diff = (i - j).astype(F32)
        decay[...] = jnp.where(diff >= 0, jnp.exp(lg * jnp.maximum(diff, 0.0)), 0.0)
        r = lax.broadcasted_iota(jnp.int32, (c, RET_D), 0).astype(F32)
        qdec[...] = jnp.exp(lg * (r + 1.0))
        kdec[...] = jnp.exp(lg * (c - 1.0 - r))

    half = RET_D // 2
    cdec = jnp.exp(jnp.zeros((1, RET_D), F32) + lg * c)
    st = state[...]
    for ci in range(RET_CHUNKS_PER_STEP):
        rs = slice(ci * c, (ci + 1) * c)
        cos = cos_ref[rs, :]
        sin = sin_ref[rs, :]

        def rot(t):
            t1, t2 = t[:, :half], t[:, half:]
            return jnp.concatenate([t1 * cos - t2 * sin, t1 * sin + t2 * cos], axis=1)

        q = rot(q_ref[rs, :].astype(F32))
        k = rot(k_ref[rs, :].astype(F32)) * (RET_D ** -0.5)
        vb = v_ref[rs, :].astype(BF16)
        qb = q.astype(BF16)
        scores = lax.dot_general(qb, k.astype(BF16), NT_DIMS, preferred_element_type=F32) * decay[...]
        o = (jnp.dot(scores.astype(BF16), vb, preferred_element_type=F32)
             + jnp.dot(qb, st.astype(BF16), preferred_element_type=F32) * qdec[...])
        kd = (k * kdec[...]).astype(BF16)
        st = st * cdec + lax.dot_general(kd, vb, TN_DIMS, preferred_element_type=F32)

        mu = jnp.mean(o, axis=-1, keepdims=True)
        oc = o - mu
        var = jnp.mean(oc * oc, axis=-1, keepdims=True)
        y = oc * lax.rsqrt(var + EPS) * gain_ref[...]
        o_ref[rs, :] = (_silu(g_ref[rs, :].astype(F32)) * y).astype(o_ref.dtype)
    state[...] = st


def retention_heads(proj, cos, sin, gain):
    s = proj.shape[0]
    c = RET_CHUNK
    log_gamma = np.log1p(-np.exp2(-5.0 - np.arange(RET_HEADS, dtype=np.float32))).astype(np.float32)
    col = lambda base: (lambda h, n, lg: (n, base + h))
    rows = min(c * RET_CHUNKS_PER_STEP, s)
    grid_spec = pltpu.PrefetchScalarGridSpec(
        num_scalar_prefetch=1,
        grid=(RET_HEADS, s // rows),
        in_specs=[pl.BlockSpec((rows, RET_D), col(0)),
                  pl.BlockSpec((rows, RET_D), col(RET_HEADS)),
                  pl.BlockSpec((rows, RET_D), col(2 * RET_HEADS)),
                  pl.BlockSpec((rows, RET_D), col(3 * RET_HEADS)),
                  pl.BlockSpec((rows, RET_D // 2), lambda h, n, lg: (n, 0)),
                  pl.BlockSpec((rows, RET_D // 2), lambda h, n, lg: (n, 0)),
                  pl.BlockSpec((1, RET_D), lambda h, n, lg: (0, h))],
        out_specs=pl.BlockSpec((rows, RET_D), lambda h, n, lg: (n, h)),
        scratch_shapes=[pltpu.VMEM((RET_D, RET_D), F32),
                        pltpu.VMEM((c, c), F32),
                        pltpu.VMEM((c, RET_D), F32),
                        pltpu.VMEM((c, RET_D), F32)],
    )
    return pl.pallas_call(
        _ret_kernel,
        grid_spec=grid_spec,
        out_shape=jax.ShapeDtypeStruct((s, RET_WIDTH), BF16),
        compiler_params=_params("arbitrary", "arbitrary"),
        name="retention",
    )(jnp.asarray(log_gamma), proj, proj, proj, proj, cos, sin, gain.reshape(1, RET_WIDTH))


def _split3(x):
    hi = x.astype(BF16)
    r1 = x - hi.astype(F32)
    mid = r1.astype(BF16)
    lo = (r1 - mid.astype(F32)).astype(BF16)
    return hi, mid, lo


def _gla_kernel(q_ref, k_ref, v_ref, g_ref, a_ref, wup_ref, b_ref, gain_ref, o_ref,
                state_t, pair_level, tri, attn_s):
    c = GLA_CHUNK

    @pl.when(pl.program_id(1) == 0)
    def _():
        state_t[...] = jnp.zeros_like(state_t)
        i = lax.broadcasted_iota(jnp.int32, (c, c), 0)
        j = lax.broadcasted_iota(jnp.int32, (c, c), 1)
        tri[...] = jnp.where(i >= j, 1.0, 0.0).astype(BF16)
        x = jnp.where(i > j, jnp.bitwise_xor(i, j), 0)
        lvl = jnp.full((c, c), GLA_LEVELS[-1] - 1, jnp.int32)
        for p in GLA_LEVELS:
            lvl = lvl + jnp.where(x >= (1 << p), 1, 0)
        pair_level[...] = lvl

    w_parts = _split3(wup_ref[...])
    st = state_t[...]
    for ci in range(GLA_CHUNKS_PER_STEP):
        cs = slice(ci * c, (ci + 1) * c)
        st = _gla_chunk(q_ref[cs, :], k_ref[cs, :], v_ref[cs, :], g_ref[cs, :], a_ref[cs, :],
                        w_parts, b_ref[...], gain_ref[...], o_ref.at[cs, :], st,
                        pair_level, tri, attn_s)
    state_t[...] = st


def _gla_chunk(q, k, v, g, a, w_parts, b, gain, o_ref, st, pair_level, tri, attn_s):
    c = GLA_CHUNK
    log2e = 1.4426950408889634
    a_hi, a_mid, a_lo = _split3(a)
    w_hi, w_mid, w_lo = w_parts
    dotf = lambda x, y: jnp.dot(x, y, preferred_element_type=F32)
    z = (dotf(a_hi, w_hi) + (dotf(a_hi, w_mid) + dotf(a_mid, w_hi))
         + (dotf(a_hi, w_lo) + dotf(a_lo, w_hi) + dotf(a_mid, w_mid))) + b
    la = (jnp.minimum(z, 0.0) - jnp.log1p(jnp.exp(-jnp.abs(z)))) * (log2e / GLA_GATE_TAU)
    l_hi, l_mid, l_lo = _split3(la)
    t = tri[...]
    cum = dotf(t, l_hi) + dotf(t, l_mid) + dotf(t, l_lo)

    qs = q.astype(F32) * (GLA_DK ** -0.5)
    k = k.astype(F32)
    vb = v.astype(BF16)

    o = lax.dot_general((qs * jnp.exp2(cum)).astype(BF16), st.astype(BF16), NT_DIMS,
                        preferred_element_type=F32)

    attn_s[...] = jnp.zeros_like(attn_s)
    for p in GLA_LEVELS:
        half = 1 << p
        nb = c // (2 * half)
        q_rows, k_rows = [], []
        for blk in range(nb):
            up = slice(blk * 2 * half, blk * 2 * half + half)
            lo = slice(blk * 2 * half + half, (blk + 1) * 2 * half)
            bnd = cum[up.stop - 1:up.stop, :]
            e_lo = jnp.exp2(cum[lo] - bnd)
            k_rows += [k[up] * jnp.exp2(bnd - cum[up]), k[lo] * e_lo]
            q_rows.append(qs[lo] * e_lo)
        q_t = q_rows[0] if nb == 1 else jnp.concatenate(q_rows, axis=0)
        k_t = jnp.concatenate(k_rows, axis=0)
        part = lax.dot_general(q_t.astype(BF16), k_t.astype(BF16), NT_DIMS,
                               preferred_element_type=F32)
        for blk in range(nb):
            lo = slice(blk * 2 * half + half, (blk + 1) * 2 * half)
            attn_s[lo, :] = jnp.where(pair_level[lo, :] == p,
                                      part[blk * half:(blk + 1) * half, :], attn_s[lo, :])

    rows = lax.broadcasted_iota(jnp.int32, (SUBLANES, 1), 0)
    lane = lax.broadcasted_iota(jnp.int32, (SUBLANES, LANES), 1)
    for blk in range(c // SUBLANES):
        r = blk * SUBLANES
        rs = slice(r, r + SUBLANES)
        ls = slice(r // LANES * LANES, r // LANES * LANES + LANES)
        q8, k8, c8 = qs[rs], k[rs], cum[rs]
        tile = attn_s[rs, ls]
        for m in range(SUBLANES):
            w = jnp.exp2(c8 - c8[m:m + 1, :])
            a = jnp.sum(q8 * w * k8[m:m + 1, :], axis=-1, keepdims=True)
            a = jnp.where(rows >= m, a, 0.0)
            tile = jnp.where(lane == r % LANES + m, a, tile)
        attn_s[rs, ls] = tile
    o = o + jnp.dot(attn_s[...].astype(BF16), vb, preferred_element_type=F32)

    last = cum[c - 1:c, :]
    kd = (k * jnp.exp2(last - cum)).astype(BF16)
    new_st = st * jnp.exp2(last) + lax.dot_general(vb, kd, TN_DIMS, preferred_element_type=F32)

    ms = jnp.mean(o * o, axis=-1, keepdims=True)
    y = o * lax.rsqrt(ms + EPS) * gain
    o_ref[...] = (_silu(g.astype(F32)) * y).astype(o_ref.dtype)
    return new_st


def gla_heads(proj, ga, w_up, b, gain):
    s = proj.shape[0]
    c = GLA_CHUNK
    qk_base = 4 * RET_WIDTH // GLA_DK
    v_base = (4 * RET_WIDTH + 2 * GLA_KEY_WIDTH) // GLA_DV
    w_up_pad = jnp.zeros((GATE_PAD, GLA_KEY_WIDTH), F32).at[:GLA_GATE_RANK].set(w_up)
    rows = c * GLA_CHUNKS_PER_STEP
    return pl.pallas_call(
        _gla_kernel,
        grid=(GLA_HEADS, s // rows),
        in_specs=[pl.BlockSpec((rows, GLA_DK), lambda h, n: (n, qk_base + h)),
                  pl.BlockSpec((rows, GLA_DK), lambda h, n: (n, qk_base + GLA_HEADS + h)),
                  pl.BlockSpec((rows, GLA_DV), lambda h, n: (n, v_base + h)),
                  pl.BlockSpec((rows, GLA_DV), lambda h, n: (n, v_base + GLA_HEADS + h)),
                  pl.BlockSpec((rows, GATE_PAD), lambda h, n: (n, 0)),
                  pl.BlockSpec((GATE_PAD, GLA_DK), lambda h, n: (0, h)),
                  pl.BlockSpec((1, GLA_DK), lambda h, n: (0, h)),
                  pl.BlockSpec((1, GLA_DV), lambda h, n: (0, h))],
        out_specs=pl.BlockSpec((rows, GLA_DV), lambda h, n: (n, h)),
        out_shape=jax.ShapeDtypeStruct((s, GLA_WIDTH), BF16),
        scratch_shapes=[pltpu.VMEM((GLA_DV, GLA_DK), F32),
                        pltpu.VMEM((c, c), jnp.int32),
                        pltpu.VMEM((c, c), BF16),
                        pltpu.VMEM((c, c), F32)],
        compiler_params=_params("arbitrary", "arbitrary"),
        name="gla",
    )(proj, proj, proj, proj, ga, w_up_pad, b.reshape(1, GLA_KEY_WIDTH), gain.reshape(1, GLA_WIDTH))


def kernel(x, positions, mix_norm, w_in, gla_w_up, gla_b, ret_gain, gla_gain, w_out, ffn_norm,
           w_gate, w_up, w_down, final_norm):
    b, s, d = x.shape
    x = x.reshape(b * s, d)
    cos, sin = rope_tables(positions.reshape(b * s))
    w_in_t = jnp.swapaxes(w_in, 1, 2)
    xg, r = norm_inputs(x, mix_norm[0])
    for l in range(DEPTH):
        proj, ga, wd = in_proj(xg, r, w_in_t, w_down, l)
        r_out = retention_heads(proj, cos, sin, ret_gain[l])
        g_out = gla_heads(proj, ga, gla_w_up[l], gla_b[l], gla_gain[l])
        x, xg, r = matmul_resid((r_out, g_out), w_out, x, layer=l, next_gain=ffn_norm[l],
                                tm=2048, single_buffer_a=True, row_splits=4, name="out_proj")
        hid = ffn_up(xg, r, w_gate, w_up, l)
        if l + 1 < DEPTH:
            x, xg, r = matmul_resid((hid,), wd, x, next_gain=mix_norm[l + 1], tm=512, name="ffn_down")
        else:
            x = matmul_resid((hid,), wd, x, tm=512, name="ffn_down")
    return rmsnorm(x, final_norm, F32).reshape(b, s, d)
```

```python
import functools

import numpy as np
import jax
import jax.numpy as jnp
from jax import lax
from jax.experimental import pallas as pl
from jax.experimental.pallas import tpu as pltpu

F32 = jnp.float32
BF16 = jnp.bfloat16

D_MODEL = 4096
DEPTH = 2
RET_HEADS = 8
RET_D = 256
RET_WIDTH = RET_HEADS * RET_D
GLA_HEADS = 4
GLA_DK = 256
GLA_DV = 512
GLA_KEY_WIDTH = GLA_HEADS * GLA_DK
GLA_WIDTH = GLA_HEADS * GLA_DV
GLA_GATE_RANK = 16
GLA_GATE_TAU = 16.0
FFN_HIDDEN = 11008
ROPE_BASE = 10000.0
EPS = 1e-6

MAIN_WIDTH = 4 * RET_WIDTH + 2 * GLA_KEY_WIDTH + 2 * GLA_WIDTH
LANES = 128
SUBLANES = 8
GATE_PAD = LANES

RET_CHUNK = 256
RET_CHUNKS_PER_STEP = 8
GLA_CHUNK = 256
GLA_CHUNKS_PER_STEP = 4
GLA_LEVELS = (7, 6, 5, 4, 3)

VMEM_LIMIT = 62 * 1024 * 1024

NT_DIMS = (((1,), (1,)), ((), ()))
TN_DIMS = (((0,), (0,)), ((), ()))


def _params(*sem):
    return pltpu.CompilerParams(dimension_semantics=sem, vmem_limit_bytes=VMEM_LIMIT)


def _silu(x):
    return x * (1.0 / (1.0 + jnp.exp(-x)))


def _rmsnorm_kernel(x_ref, g_ref, o_ref):
    x = x_ref[...]
    ms = jnp.mean(x * x, axis=-1, keepdims=True)
    o_ref[...] = (x * lax.rsqrt(ms + EPS) * g_ref[...]).astype(o_ref.dtype)


def rmsnorm(x, gain, out_dtype, tm=256):
    s, d = x.shape
    return pl.pallas_call(
        _rmsnorm_kernel,
        grid=(s // tm,),
        in_specs=[pl.BlockSpec((tm, d), lambda i: (i, 0)),
                  pl.BlockSpec((1, d), lambda i: (0, 0))],
        out_specs=pl.BlockSpec((tm, d), lambda i: (i, 0)),
        out_shape=jax.ShapeDtypeStruct((s, d), out_dtype),
        compiler_params=_params("arbitrary"),
        name="rmsnorm",
    )(x, gain.reshape(1, d))


def _fold_lanes(v):
    acc = v[:, :LANES]
    for g in range(1, v.shape[1] // LANES):
        acc = acc + v[:, g * LANES:(g + 1) * LANES]
    return acc


def _row_scale(ss):
    r = lax.rsqrt(jnp.sum(ss, axis=-1, keepdims=True) * (1.0 / D_MODEL) + EPS)
    return jnp.broadcast_to(r, ss.shape)


def _scale_rows(acc, r):
    return acc * jnp.concatenate([r] * (acc.shape[1] // LANES), axis=1)


def _norm_inputs_kernel(x_ref, g_ref, xg_ref, r_ref):
    x = x_ref[...]
    xg_ref[...] = (x * g_ref[...]).astype(BF16)
    r_ref[...] = _row_scale(_fold_lanes(x * x))


def norm_inputs(x, gain, tm=256):
    s, d = x.shape
    return pl.pallas_call(
        _norm_inputs_kernel,
        grid=(s // tm,),
        in_specs=[pl.BlockSpec((tm, d), lambda i: (i, 0)),
                  pl.BlockSpec((1, d), lambda i: (0, 0))],
        out_specs=[pl.BlockSpec((tm, d), lambda i: (i, 0)),
                   pl.BlockSpec((tm, LANES), lambda i: (i, 0))],
        out_shape=[jax.ShapeDtypeStruct((s, d), BF16),
                   jax.ShapeDtypeStruct((s, LANES), F32)],
        compiler_params=_params("arbitrary"),
        name="norm_inputs",
    )(x, gain.reshape(1, d))


def _row_block_spec(tm, k, single_buffer=True):
    return pl.BlockSpec((tm, k), lambda i, j: (i, 0),
                        pipeline_mode=pl.Buffered(1) if single_buffer else None)


def _mm_resid_kernel(*all_refs, n_parts, row_splits):
    operand_refs, res_ref, refs = all_refs[:2 * n_parts], all_refs[2 * n_parts], all_refs[2 * n_parts + 1:]
    emit_norm = len(refs) > 1
    if emit_norm:
        g_ref, o_ref, xg_ref, r_ref = refs
        j = pl.program_id(1)

        @pl.when(j == 0)
        def _():
            r_ref[...] = jnp.zeros_like(r_ref)
    else:
        o_ref, = refs

    ws = [operand_refs[2 * p + 1][...].astype(BF16) for p in range(n_parts)]
    rows = o_ref.shape[0] // row_splits
    for s in range(row_splits):
        rs = slice(s * rows, (s + 1) * rows)
        x = res_ref[rs, :]
        for p in range(n_parts):
            x = x + jnp.dot(operand_refs[2 * p][rs, :], ws[p], preferred_element_type=F32)
        o_ref[rs, :] = x
        if emit_norm:
            xg_ref[rs, :] = (x * g_ref[...]).astype(BF16)
            r_ref[rs, :] += _fold_lanes(x * x)

    if emit_norm:
        @pl.when(j == pl.num_programs(1) - 1)
        def _():
            r_ref[...] = _row_scale(r_ref[...])


def matmul_resid(a_parts, w, resid, layer=None, next_gain=None, tm=1024, tn=512,
                 single_buffer_a=False, row_splits=1, name="matmul"):
    m, k = a_parts[0].shape
    n = w.shape[-1]
    tile = pl.BlockSpec((tm, tn), lambda i, j: (i, j))
    in_specs, args = [], []
    for p, a in enumerate(a_parts):
        in_specs.append(_row_block_spec(tm, k, single_buffer_a))
        if w.ndim == 3:
            in_specs.append(pl.BlockSpec((None, k, tn), lambda i, j, p=p: (layer, p, j)))
        else:
            in_specs.append(pl.BlockSpec((k, tn), lambda i, j, p=p: (p, j)))
        args += [a, w]
    in_specs.append(tile)
    args.append(resid)
    out_specs, out_shape = [tile], [jax.ShapeDtypeStruct((m, n), F32)]
    if next_gain is not None:
        in_specs.append(pl.BlockSpec((1, tn), lambda i, j: (0, j)))
        args.append(next_gain.reshape(1, n))
        out_specs += [tile, pl.BlockSpec((tm, LANES), lambda i, j: (i, 0))]
        out_shape += [jax.ShapeDtypeStruct((m, n), BF16), jax.ShapeDtypeStruct((m, LANES), F32)]
    outs = pl.pallas_call(
        functools.partial(_mm_resid_kernel, n_parts=len(a_parts), row_splits=row_splits),
        grid=(m // tm, n // tn),
        in_specs=in_specs,
        out_specs=out_specs,
        out_shape=out_shape,
        compiler_params=_params("arbitrary", "arbitrary"),
        name=name,
    )(*args)
    return outs if next_gain is not None else outs[0]


def _in_proj_kernel(xg_ref, r_ref, w_ref, wlr_ref, wd_ref, o_ref, ga_ref, wd_bf16_ref):
    acc = lax.dot_general(xg_ref[...], w_ref[...].astype(BF16), NT_DIMS, preferred_element_type=F32)
    o_ref[...] = _scale_rows(acc, r_ref[...]).astype(o_ref.dtype)
    wd_bf16_ref[...] = wd_ref[...].astype(BF16)

    @pl.when(pl.program_id(1) == 0)
    def _():
        wlr = wlr_ref[...]
        row = lax.broadcasted_iota(jnp.int32, wlr.shape, 0)
        wlr = jnp.where(row < GLA_GATE_RANK, wlr, 0.0)
        ga = lax.dot_general(xg_ref[...], wlr.astype(BF16), NT_DIMS, preferred_element_type=F32)
        ga_ref[...] = ga * r_ref[...]


def in_proj(xg, r, w_in_t, w_down, layer, tm=2048, tn=512, tr=128):
    m, k = xg.shape
    _, f, d = w_down.shape
    nj = MAIN_WIDTH // tn
    n_wd = f // tr
    assert n_wd * tr == f and n_wd <= (m // tm) * nj
    wd_block = lambda i, j: jnp.minimum(i * nj + j, n_wd - 1)
    return pl.pallas_call(
        _in_proj_kernel,
        grid=(m // tm, nj),
        in_specs=[_row_block_spec(tm, k),
                  _row_block_spec(tm, LANES),
                  pl.BlockSpec((None, tn, k), lambda i, j: (layer, j, 0)),
                  pl.BlockSpec((None, GATE_PAD, k), lambda i, j: (layer, MAIN_WIDTH // GATE_PAD, 0)),
                  pl.BlockSpec((None, tr, d), lambda i, j: (layer, wd_block(i, j), 0))],
        out_specs=[pl.BlockSpec((tm, tn), lambda i, j: (i, j)),
                   pl.BlockSpec((tm, GATE_PAD), lambda i, j: (i, 0)),
                   pl.BlockSpec((tr, d), lambda i, j: (wd_block(i, j), 0))],
        out_shape=[jax.ShapeDtypeStruct((m, MAIN_WIDTH), BF16),
                   jax.ShapeDtypeStruct((m, GATE_PAD), F32),
                   jax.ShapeDtypeStruct((f, d), BF16)],
        compiler_params=_params("arbitrary", "arbitrary"),
        name="in_proj",
    )(xg, r, w_in_t, w_in_t, w_down)


def _ffn_up_kernel(xg_ref, r_ref, wg_ref, wu_ref, o_ref, *, row_splits):
    wg = wg_ref[...].astype(BF16)
    wu = wu_ref[...].astype(BF16)
    rows = xg_ref.shape[0] // row_splits
    for s in range(row_splits):
        rs = slice(s * rows, (s + 1) * rows)
        g = _scale_rows(jnp.dot(xg_ref[rs, :], wg, preferred_element_type=F32), r_ref[rs, :])
        u = _scale_rows(jnp.dot(xg_ref[rs, :], wu, preferred_element_type=F32), r_ref[rs, :])
        o_ref[rs, :] = (_silu(g) * u).astype(o_ref.dtype)


def ffn_up(xg, r, w_gate, w_up, layer, tm=4096, tn=256, row_splits=4):
    m, k = xg.shape
    n = w_gate.shape[2]
    w_spec = pl.BlockSpec((None, k, tn), lambda i, j: (layer, 0, j))
    return pl.pallas_call(
        functools.partial(_ffn_up_kernel, row_splits=row_splits),
        grid=(m // tm, n // tn),
        in_specs=[_row_block_spec(tm, k), _row_block_spec(tm, LANES), w_spec, w_spec],
        out_specs=pl.BlockSpec((tm, tn), lambda i, j: (i, j)),
        out_shape=jax.ShapeDtypeStruct((m, n), BF16),
        compiler_params=_params("arbitrary", "arbitrary"),
        name="ffn_up",
    )(xg, r, w_gate, w_up)


def _rope_kernel(pos_ref, inv_ref, cos_ref, sin_ref):
    ang = pos_ref[...].astype(F32) * inv_ref[...]
    cos_ref[...] = jnp.cos(ang)
    sin_ref[...] = jnp.sin(ang)


def rope_tables(positions, tm=1024):
    s = positions.shape[0]
    tm = min(tm, s)
    half = RET_D // 2
    inv_freq = (ROPE_BASE ** (-np.arange(half, dtype=np.float32) / half)).astype(np.float32)
    return pl.pallas_call(
        _rope_kernel,
        grid=(s // tm,),
        in_specs=[pl.BlockSpec((tm, 1), lambda i: (i, 0)),
                  pl.BlockSpec((1, half), lambda i: (0, 0))],
        out_specs=[pl.BlockSpec((tm, half), lambda i: (i, 0))] * 2,
        out_shape=[jax.ShapeDtypeStruct((s, half), F32)] * 2,
        compiler_params=_params("arbitrary"),
        name="rope_tables",
    )(positions.reshape(s, 1), jnp.asarray(inv_freq).reshape(1, half))


def _ret_kernel(lg_ref, q_ref, k_ref, v_ref, g_ref, cos_ref, sin_ref, gain_ref, o_ref,
                state, decay, qdec, kdec):
    c = RET_CHUNK
    lg = lg_ref[pl.program_id(0)]

    @pl.when(pl.program_id(1) == 0)
    def _():
        state[...] = jnp.zeros_like(state)
        i = lax.broadcasted_iota(jnp.int32, (c, c), 0)
        j = lax.broadcasted_iota(jnp.int32, (c, c), 1)
        diff = (i - j).astype(F32)
        decay[...] = jnp.where(diff >= 0, jnp.exp(lg * jnp.maximum(diff, 0.0)), 0.0)
        r = lax.broadcasted_iota(jnp.int32, (c, RET_D), 0).astype(F32)
        qdec[...] = jnp.exp(lg * (r + 1.0))
        kdec[...] = jnp.exp(lg * (c - 1.0 - r))

    half = RET_D // 2
    cdec = jnp.exp(jnp.zeros((1, RET_D), F32) + lg * c)
    st = state[...]
    for ci in range(RET_CHUNKS_PER_STEP):
        rs = slice(ci * c, (ci + 1) * c)
        cos = cos_ref[rs, :]
        sin = sin_ref[rs, :]

        def rot(t):
            t1, t2 = t[:, :half], t[:, half:]
            return jnp.concatenate([t1 * cos - t2 * sin, t1 * sin + t2 * cos], axis=1)

        q = rot(q_ref[rs, :].astype(F32))
        k = rot(k_ref[rs, :].astype(F32)) * (RET_D ** -0.5)
        vb = v_ref[rs, :].astype(BF16)
        qb = q.astype(BF16)
        scores = lax.dot_general(qb, k.astype(BF16), NT_DIMS, preferred_element_type=F32) * decay[...]
        o = (jnp.dot(scores.astype(BF16), vb, preferred_element_type=F32)
             + jnp.dot(qb, st.astype(BF16), preferred_element_type=F32) * qdec[...])
        kd = (k * kdec[...]).astype(BF16)
        st = st * cdec + lax.dot_general(kd, vb, TN_DIMS, preferred_element_type=F32)

        mu = jnp.mean(o, axis=-1, keepdims=True)
        oc = o - mu
        var = jnp.mean(oc * oc, axis=-1, keepdims=True)
        y = oc * lax.rsqrt(var + EPS) * gain_ref[...]
        o_ref[rs, :] = (_silu(g_ref[rs, :].astype(F32)) * y).astype(o_ref.dtype)
    state[...] = st


def retention_heads(proj, cos, sin, gain):
    s = proj.shape[0]
    c = RET_CHUNK
    log_gamma = np.log1p(-np.exp2(-5.0 - np.arange(RET_HEADS, dtype=np.float32))).astype(np.float32)
    col = lambda base: (lambda h, n, lg: (n, base + h))
    rows = min(c * RET_CHUNKS_PER_STEP, s)
    grid_spec = pltpu.PrefetchScalarGridSpec(
        num_scalar_prefetch=1,
        grid=(RET_HEADS, s // rows),
        in_specs=[pl.BlockSpec((rows, RET_D), col(0)),
                  pl.BlockSpec((rows, RET_D), col(RET_HEADS)),
                  pl.BlockSpec((rows, RET_D), col(2 * RET_HEADS)),
                  pl.BlockSpec((rows, RET_D), col(3 * RET_HEADS)),
                  pl.BlockSpec((rows, RET_D // 2), lambda h, n, lg: (n, 0)),
                  pl.BlockSpec((rows, RET_D // 2), lambda h, n, lg: (n, 0)),
                  pl.BlockSpec((1, RET_D), lambda h, n, lg: (0, h))],
        out_specs=pl.BlockSpec((rows, RET_D), lambda h, n, lg: (n, h)),
        scratch_shapes=[pltpu.VMEM((RET_D, RET_D), F32),
                        pltpu.VMEM((c, c), F32),
                        pltpu.VMEM((c, RET_D), F32),
                        pltpu.VMEM((c, RET_D), F32)],
    )
    return pl.pallas_call(
        _ret_kernel,
        grid_spec=grid_spec,
        out_shape=jax.ShapeDtypeStruct((s, RET_WIDTH), BF16),
        compiler_params=_params("arbitrary", "arbitrary"),
        name="retention",
    )(jnp.asarray(log_gamma), proj, proj, proj, proj, cos, sin, gain.reshape(1, RET_WIDTH))


def _split3(x):
    hi = x.astype(BF16)
    r1 = x - hi.astype(F32)
    mid = r1.astype(BF16)
    lo = (r1 - mid.astype(F32)).astype(BF16)
    return hi, mid, lo


def _gla_kernel(q_ref, k_ref, v_ref, g_ref, a_ref, wup_ref, b_ref, gain_ref, o_ref,
                state_t, pair_level, tri, attn_s):
    c = GLA_CHUNK

    @pl.when(pl.program_id(1) == 0)
    def _():
        state_t[...] = jnp.zeros_like(state_t)
        i = lax.broadcasted_iota(jnp.int32, (c, c), 0)
        j = lax.broadcasted_iota(jnp.int32, (c, c), 1)
        tri[...] = jnp.where(i >= j, 1.0, 0.0).astype(BF16)
        x = jnp.where(i > j, jnp.bitwise_xor(i, j), 0)
        lvl = jnp.full((c, c), GLA_LEVELS[-1] - 1, jnp.int32)
        for p in GLA_LEVELS:
            lvl = lvl + jnp.where(x >= (1 << p), 1, 0)
        pair_level[...] = lvl

    w_parts = _split3(wup_ref[...])
    st = state_t[...]
    for ci in range(GLA_CHUNKS_PER_STEP):
        cs = slice(ci * c, (ci + 1) * c)
        st = _gla_chunk(q_ref[cs, :], k_ref[cs, :], v_ref[cs, :], g_ref[cs, :], a_ref[cs, :],
                        w_parts, b_ref[...], gain_ref[...], o_ref.at[cs, :], st,
                        pair_level, tri, attn_s)
    state_t[...] = st


def _gla_chunk(q, k, v, g, a, w_parts, b, gain, o_ref, st, pair_level, tri, attn_s):
    c = GLA_CHUNK
    log2e = 1.4426950408889634
    a_hi, a_mid, a_lo = _split3(a)
    w_hi, w_mid, w_lo = w_parts
    dotf = lambda x, y: jnp.dot(x, y, preferred_element_type=F32)
    z = (dotf(a_hi, w_hi) + (dotf(a_hi, w_mid) + dotf(a_mid, w_hi))
         + (dotf(a_hi, w_lo) + dotf(a_lo, w_hi) + dotf(a_mid, w_mid))) + b
    la = (jnp.minimum(z, 0.0) - jnp.log1p(jnp.exp(-jnp.abs(z)))) * (log2e / GLA_GATE_TAU)
    l_hi, l_mid, l_lo = _split3(la)
    t = tri[...]
    cum = dotf(t, l_hi) + dotf(t, l_mid) + dotf(t, l_lo)

    qs = q.astype(F32) * (GLA_DK ** -0.5)
    k = k.astype(F32)
    vb = v.astype(BF16)

    o = lax.dot_general((qs * jnp.exp2(cum)).astype(BF16), st.astype(BF16), NT_DIMS,
                        preferred_element_type=F32)

    attn_s[...] = jnp.zeros_like(attn_s)
    for p in GLA_LEVELS:
        half = 1 << p
        nb = c // (2 * half)
        q_rows, k_rows = [], []
        for blk in range(nb):
            up = slice(blk * 2 * half, blk * 2 * half + half)
            lo = slice(blk * 2 * half + half, (blk + 1) * 2 * half)
            bnd = cum[up.stop - 1:up.stop, :]
            e_lo = jnp.exp2(cum[lo] - bnd)
            k_rows += [k[up] * jnp.exp2(bnd - cum[up]), k[lo] * e_lo]
            q_rows.append(qs[lo] * e_lo)
        q_t = q_rows[0] if nb == 1 else jnp.concatenate(q_rows, axis=0)
        k_t = jnp.concatenate(k_rows, axis=0)
        part = lax.dot_general(q_t.astype(BF16), k_t.astype(BF16), NT_DIMS,
                               preferred_element_type=F32)
        for blk in range(nb):
            lo = slice(blk * 2 * half + half, (blk + 1) * 2 * half)
            attn_s[lo, :] = jnp.where(pair_level[lo, :] == p,
                                      part[blk * half:(blk + 1) * half, :], attn_s[lo, :])

    rows = lax.broadcasted_iota(jnp.int32, (SUBLANES, 1), 0)
    lane = lax.broadcasted_iota(jnp.int32, (SUBLANES, LANES), 1)
    for blk in range(c // SUBLANES):
        r = blk * SUBLANES
        rs = slice(r, r + SUBLANES)
        ls = slice(r // LANES * LANES, r // LANES * LANES + LANES)
        q8, k8, c8 = qs[rs], k[rs], cum[rs]
        tile = attn_s[rs, ls]
        for m in range(SUBLANES):
            w = jnp.exp2(c8 - c8[m:m + 1, :])
            a = jnp.sum(q8 * w * k8[m:m + 1, :], axis=-1, keepdims=True)
            a = jnp.where(rows >= m, a, 0.0)
            tile = jnp.where(lane == r % LANES + m, a, tile)
        attn_s[rs, ls] = tile
    o = o + jnp.dot(attn_s[...].astype(BF16), vb, preferred_element_type=F32)

    last = cum[c - 1:c, :]
    kd = (k * jnp.exp2(last - cum)).astype(BF16)
    new_st = st * jnp.exp2(last) + lax.dot_general(vb, kd, TN_DIMS, preferred_element_type=F32)

    ms = jnp.mean(o * o, axis=-1, keepdims=True)
    y = o * lax.rsqrt(ms + EPS) * gain
    o_ref[...] = (_silu(g.astype(F32)) * y).astype(o_ref.dtype)
    return new_st


def gla_heads(proj, ga, w_up, b, gain):
    s = proj.shape[0]
    c = GLA_CHUNK
    qk_base = 4 * RET_WIDTH // GLA_DK
    v_base = (4 * RET_WIDTH + 2 * GLA_KEY_WIDTH) // GLA_DV
    w_up_pad = jnp.zeros((GATE_PAD, GLA_KEY_WIDTH), F32).at[:GLA_GATE_RANK].set(w_up)
    rows = c * GLA_CHUNKS_PER_STEP
    return pl.pallas_call(
        _gla_kernel,
        grid=(GLA_HEADS, s // rows),
        in_specs=[pl.BlockSpec((rows, GLA_DK), lambda h, n: (n, qk_base + h)),
                  pl.BlockSpec((rows, GLA_DK), lambda h, n: (n, qk_base + GLA_HEADS + h)),
                  pl.BlockSpec((rows, GLA_DV), lambda h, n: (n, v_base + h)),
                  pl.BlockSpec((rows, GLA_DV), lambda h, n: (n, v_base + GLA_HEADS + h)),
                  pl.BlockSpec((rows, GATE_PAD), lambda h, n: (n, 0)),
                  pl.BlockSpec((GATE_PAD, GLA_DK), lambda h, n: (0, h)),
                  pl.BlockSpec((1, GLA_DK), lambda h, n: (0, h)),
                  pl.BlockSpec((1, GLA_DV), lambda h, n: (0, h))],
        out_specs=pl.BlockSpec((rows, GLA_DV), lambda h, n: (n, h)),
        out_shape=jax.ShapeDtypeStruct((s, GLA_WIDTH), BF16),
        scratch_shapes=[pltpu.VMEM((GLA_DV, GLA_DK), F32),
                        pltpu.VMEM((c, c), jnp.int32),
                        pltpu.VMEM((c, c), BF16),
                        pltpu.VMEM((c, c), F32)],
        compiler_params=_params("arbitrary", "arbitrary"),
        name="gla",
    )(proj, proj, proj, proj, ga, w_up_pad, b.reshape(1, GLA_KEY_WIDTH), gain.reshape(1, GLA_WIDTH))


def kernel(x, positions, mix_norm, w_in, gla_w_up, gla_b, ret_gain, gla_gain, w_out, ffn_norm,
           w_gate, w_up, w_down, final_norm):
    b, s, d = x.shape
    x = x.reshape(b * s, d)
    cos, sin = rope_tables(positions.reshape(b * s))
    w_in_t = jnp.swapaxes(w_in, 1, 2)
    xg, r = norm_inputs(x, mix_norm[0])
    for l in range(DEPTH):
        proj, ga, wd = in_proj(xg, r, w_in_t, w_down, l)
        r_out = retention_heads(proj, cos, sin, ret_gain[l])
        g_out = gla_heads(proj, ga, gla_w_up[l], gla_b[l], gla_gain[l])
        x, xg, r = matmul_resid((r_out, g_out), w_out, x, layer=l, next_gain=ffn_norm[l],
                                tm=2048, single_buffer_a=True, row_splits=4, name="out_proj")
        hid = ffn_up(xg, r, w_gate, w_up, l)
        if l + 1 < DEPTH:
            x, xg, r = matmul_resid((hid,), wd, x, next_gain=mix_norm[l + 1], tm=512, name="ffn_down")
        else:
            x = matmul_resid((hid,), wd, x, tm=512, name="ffn_down")
    return rmsnorm(x, final_norm, F32).reshape(b, s, d)
```

```python
import functools

import numpy as np
import jax
import jax.numpy as jnp
from jax import lax
from jax.experimental import pallas as pl
from jax.experimental.pallas import tpu as pltpu

F32 = jnp.float32
BF16 = jnp.bfloat16

D_MODEL = 4096
DEPTH = 2
RET_HEADS = 8
RET_D = 256
RET_WIDTH = RET_HEADS * RET_D
GLA_HEADS = 4
GLA_DK = 256
GLA_DV = 512
GLA_KEY_WIDTH = GLA_HEADS * GLA_DK
GLA_WIDTH = GLA_HEADS * GLA_DV
GLA_GATE_RANK = 16
GLA_GATE_TAU = 16.0
FFN_HIDDEN = 11008
ROPE_BASE = 10000.0
EPS = 1e-6

MAIN_WIDTH = 4 * RET_WIDTH + 2 * GLA_KEY_WIDTH + 2 * GLA_WIDTH
LANES = 128
SUBLANES = 8
GATE_PAD = LANES

RET_CHUNK = 256
RET_CHUNKS_PER_STEP = 8
GLA_CHUNK = 256
GLA_CHUNKS_PER_STEP = 4
GLA_LEVELS = (7, 6, 5, 4, 3)

VMEM_LIMIT = 62 * 1024 * 1024

NT_DIMS = (((1,), (1,)), ((), ()))
TN_DIMS = (((0,), (0,)), ((), ()))


def _params(*sem):
    return pltpu.CompilerParams(dimension_semantics=sem, vmem_limit_bytes=VMEM_LIMIT)


def _silu(x):
    return x * (1.0 / (1.0 + jnp.exp(-x)))


def _rmsnorm_kernel(x_ref, g_ref, o_ref):
    x = x_ref[...]
    ms = jnp.mean(x * x, axis=-1, keepdims=True)
    o_ref[...] = (x * lax.rsqrt(ms + EPS) * g_ref[...]).astype(o_ref.dtype)


def rmsnorm(x, gain, out_dtype, tm=256):
    s, d = x.shape
    return pl.pallas_call(
        _rmsnorm_kernel,
        grid=(s // tm,),
        in_specs=[pl.BlockSpec((tm, d), lambda i: (i, 0)),
                  pl.BlockSpec((1, d), lambda i: (0, 0))],
        out_specs=pl.BlockSpec((tm, d), lambda i: (i, 0)),
        out_shape=jax.ShapeDtypeStruct((s, d), out_dtype),
        compiler_params=_params("arbitrary"),
        name="rmsnorm",
    )(x, gain.reshape(1, d))


def _fold_lanes(v):
    acc = v[:, :LANES]
    for g in range(1, v.shape[1] // LANES):
        acc = acc + v[:, g * LANES:(g + 1) * LANES]
    return acc


def _row_scale(ss):
    r = lax.rsqrt(jnp.sum(ss, axis=-1, keepdims=True) * (1.0 / D_MODEL) + EPS)
    return jnp.broadcast_to(r, ss.shape)


def _scale_rows(acc, r):
    return acc * jnp.concatenate([r] * (acc.shape[1] // LANES), axis=1)


def _norm_inputs_kernel(x_ref, g_ref, xg_ref, r_ref):
    x = x_ref[...]
    xg_ref[...] = (x * g_ref[...]).astype(BF16)
    r_ref[...] = _row_scale(_fold_lanes(x * x))


def norm_inputs(x, gain, tm=256):
    s, d = x.shape
    return pl.pallas_call(
        _norm_inputs_kernel,
        grid=(s // tm,),
        in_specs=[pl.BlockSpec((tm, d), lambda i: (i, 0)),
                  pl.BlockSpec((1, d), lambda i: (0, 0))],
        out_specs=[pl.BlockSpec((tm, d), lambda i: (i, 0)),
                   pl.BlockSpec((tm, LANES), lambda i: (i, 0))],
        out_shape=[jax.ShapeDtypeStruct((s, d), BF16),
                   jax.ShapeDtypeStruct((s, LANES), F32)],
        compiler_params=_params("arbitrary"),
        name="norm_inputs",
    )(x, gain.reshape(1, d))


def _row_block_spec(tm, k, single_buffer=True):
    return pl.BlockSpec((tm, k), lambda i, j: (i, 0),
                        pipeline_mode=pl.Buffered(1) if single_buffer else None)


def _mm_resid_kernel(*all_refs, n_parts, row_splits):
    operand_refs, res_ref, refs = all_refs[:2 * n_parts], all_refs[2 * n_parts], all_refs[2 * n_parts + 1:]
    emit_norm = len(refs) > 1
    if emit_norm:
        g_ref, o_ref, xg_ref, r_ref = refs
        j = pl.program_id(1)

        @pl.when(j == 0)
        def _():
            r_ref[...] = jnp.zeros_like(r_ref)
    else:
        o_ref, = refs

    ws = [operand_refs[2 * p + 1][...].astype(BF16) for p in range(n_parts)]
    rows = o_ref.shape[0] // row_splits
    for s in range(row_splits):
        rs = slice(s * rows, (s + 1) * rows)
        x = res_ref[rs, :]
        for p in range(n_parts):
            x = x + jnp.dot(operand_refs[2 * p][rs, :], ws[p], preferred_element_type=F32)
        o_ref[rs, :] = x
        if emit_norm:
            xg_ref[rs, :] = (x * g_ref[...]).astype(BF16)
            r_ref[rs, :] += _fold_lanes(x * x)

    if emit_norm:
        @pl.when(j == pl.num_programs(1) - 1)
        def _():
            r_ref[...] = _row_scale(r_ref[...])


def matmul_resid(a_parts, w, resid, layer=None, next_gain=None, tm=1024, tn=512,
                 single_buffer_a=False, row_splits=1, name="matmul"):
    m, k = a_parts[0].shape
    n = w.shape[-1]
    tile = pl.BlockSpec((tm, tn), lambda i, j: (i, j))
    in_specs, args = [], []
    for p, a in enumerate(a_parts):
        in_specs.append(_row_block_spec(tm, k, single_buffer_a))
        if w.ndim == 3:
            in_specs.append(pl.BlockSpec((None, k, tn), lambda i, j, p=p: (layer, p, j)))
        else:
            in_specs.append(pl.BlockSpec((k, tn), lambda i, j, p=p: (p, j)))
        args += [a, w]
    in_specs.append(tile)
    args.append(resid)
    out_specs, out_shape = [tile], [jax.ShapeDtypeStruct((m, n), F32)]
    if next_gain is not None:
        in_specs.append(pl.BlockSpec((1, tn), lambda i, j: (0, j)))
        args.append(next_gain.reshape(1, n))
        out_specs += [tile, pl.BlockSpec((tm, LANES), lambda i, j: (i, 0))]
        out_shape += [jax.ShapeDtypeStruct((m, n), BF16), jax.ShapeDtypeStruct((m, LANES), F32)]
    outs = pl.pallas_call(
        functools.partial(_mm_resid_kernel, n_parts=len(a_parts), row_splits=row_splits),
        grid=(m // tm, n // tn),
        in_specs=in_specs,
        out_specs=out_specs,
        out_shape=out_shape,
        compiler_params=_params("arbitrary", "arbitrary"),
        name=name,
    )(*args)
    return outs if next_gain is not None else outs[0]


def _in_proj_kernel(xg_ref, r_ref, w_ref, wlr_ref, wd_ref, o_ref, ga_ref, wd_bf16_ref):
    acc = lax.dot_general(xg_ref[...], w_ref[...].astype(BF16), NT_DIMS, preferred_element_type=F32)
    o_ref[...] = _scale_rows(acc, r_ref[...]).astype(o_ref.dtype)
    wd_bf16_ref[...] = wd_ref[...].astype(BF16)

    @pl.when(pl.program_id(1) == 0)
    def _():
        wlr = wlr_ref[...]
        row = lax.broadcasted_iota(jnp.int32, wlr.shape, 0)
        wlr = jnp.where(row < GLA_GATE_RANK, wlr, 0.0)
        ga = lax.dot_general(xg_ref[...], wlr.astype(BF16), NT_DIMS, preferred_element_type=F32)
        ga_ref[...] = ga * r_ref[...]


def in_proj(xg, r, w_in_t, w_down, layer, tm=2048, tn=512, tr=128):
    m, k = xg.shape
    _, f, d = w_down.shape
    nj = MAIN_WIDTH // tn
    n_wd = f // tr
    assert n_wd * tr == f and n_wd <= (m // tm) * nj
    wd_block = lambda i, j: jnp.minimum(i * nj + j, n_wd - 1)
    return pl.pallas_call(
        _in_proj_kernel,
        grid=(m // tm, nj),
        in_specs=[_row_block_spec(tm, k),
                  _row_block_spec(tm, LANES),
                  pl.BlockSpec((None, tn, k), lambda i, j: (layer, j, 0)),
                  pl.BlockSpec((None, GATE_PAD, k), lambda i, j: (layer, MAIN_WIDTH // GATE_PAD, 0)),
                  pl.BlockSpec((None, tr, d), lambda i, j: (layer, wd_block(i, j), 0))],
        out_specs=[pl.BlockSpec((tm, tn), lambda i, j: (i, j)),
                   pl.BlockSpec((tm, GATE_PAD), lambda i, j: (i, 0)),
                   pl.BlockSpec((tr, d), lambda i, j: (wd_block(i, j), 0))],
        out_shape=[jax.ShapeDtypeStruct((m, MAIN_WIDTH), BF16),
                   jax.ShapeDtypeStruct((m, GATE_PAD), F32),
                   jax.ShapeDtypeStruct((f, d), BF16)],
        compiler_params=_params("arbitrary", "arbitrary"),
        name="in_proj",
    )(xg, r, w_in_t, w_in_t, w_down)


def _ffn_up_kernel(xg_ref, r_ref, wg_ref, wu_ref, o_ref, *, row_splits):
    wg = wg_ref[...].astype(BF16)
    wu = wu_ref[...].astype(BF16)
    rows = xg_ref.shape[0] // row_splits
    for s in range(row_splits):
        rs = slice(s * rows, (s + 1) * rows)
        g = _scale_rows(jnp.dot(xg_ref[rs, :], wg, preferred_element_type=F32), r_ref[rs, :])
        u = _scale_rows(jnp.dot(xg_ref[rs, :], wu, preferred_element_type=F32), r_ref[rs, :])
        o_ref[rs, :] = (_silu(g) * u).astype(o_ref.dtype)


def ffn_up(xg, r, w_gate, w_up, layer, tm=4096, tn=256, row_splits=4):
    m, k = xg.shape
    n = w_gate.shape[2]
    w_spec = pl.BlockSpec((None, k, tn), lambda i, j: (layer, 0, j))
    return pl.pallas_call(
        functools.partial(_ffn_up_kernel, row_splits=row_splits),
        grid=(m // tm, n // tn),
        in_specs=[_row_block_spec(tm, k), _row_block_spec(tm, LANES), w_spec, w_spec],
        out_specs=pl.BlockSpec((tm, tn), lambda i, j: (i, j)),
        out_shape=jax.ShapeDtypeStruct((m, n), BF16),
        compiler_params=_params("arbitrary", "arbitrary"),
        name="ffn_up",
    )(xg, r, w_gate, w_up)


def _rope_kernel(pos_ref, inv_ref, cos_ref, sin_ref):
    ang = pos_ref[...].astype(F32) * inv_ref[...]
    cos_ref[...] = jnp.cos(ang)
    sin_ref[...] = jnp.sin(ang)


def rope_tables(positions, tm=1024):
    s = positions.shape[0]
    tm = min(tm, s)
    half = RET_D // 2
    inv_freq = (ROPE_BASE ** (-np.arange(half, dtype=np.float32) / half)).astype(np.float32)
    return pl.pallas_call(
        _rope_kernel,
        grid=(s // tm,),
        in_specs=[pl.BlockSpec((tm, 1), lambda i: (i, 0)),
                  pl.BlockSpec((1, half), lambda i: (0, 0))],
        out_specs=[pl.BlockSpec((tm, half), lambda i: (i, 0))] * 2,
        out_shape=[jax.ShapeDtypeStruct((s, half), F32)] * 2,
        compiler_params=_params("arbitrary"),
        name="rope_tables",
    )(positions.reshape(s, 1), jnp.asarray(inv_freq).reshape(1, half))


def _ret_kernel(lg_ref, q_ref, k_ref, v_ref, g_ref, cos_ref, sin_ref, gain_ref, o_ref,
                state, decay, qdec, kdec):
    c = RET_CHUNK
    lg = lg_ref[pl.program_id(0)]

    @pl.when(pl.program_id(1) == 0)
    def _():
        state[...] = jnp.zeros_like(state)
        i = lax.broadcasted_iota(jnp.int32, (c, c), 0)
        j = lax.broadcasted_iota(jnp.int32, (c, c), 1)
        diff = (i - j).astype(F32)
        k_scale = RET_D ** -0.5
        decay[...] = jnp.where(diff >= 0, jnp.exp(lg * jnp.maximum(diff, 0.0)) * k_scale, 0.0)
        r = lax.broadcasted_iota(jnp.int32, (c, RET_D), 0).astype(F32)
        qdec[...] = jnp.exp(lg * (r + 1.0))
        kdec[...] = jnp.exp(lg * (c - 1.0 - r)) * k_scale

    half = RET_D // 2
    cdec = jnp.exp(jnp.zeros((1, RET_D), F32) + lg * c)
    st = state[...]
    for ci in range(RET_CHUNKS_PER_STEP):
        rs = slice(ci * c, (ci + 1) * c)
        cos = cos_ref[rs, :]
        sin = sin_ref[rs, :]

        def rot(t):
            t1, t2 = t[:, :half], t[:, half:]
            return jnp.concatenate([t1 * cos - t2 * sin, t1 * sin + t2 * cos], axis=1)

        q = rot(q_ref[rs, :].astype(F32))
        k = rot(k_ref[rs, :].astype(F32))
        vb = v_ref[rs, :].astype(BF16)
        qb = q.astype(BF16)
        scores = lax.dot_general(qb, k.astype(BF16), NT_DIMS, preferred_element_type=F32) * decay[...]
        o = (jnp.dot(scores.astype(BF16), vb, preferred_element_type=F32)
             + jnp.dot(qb, st.astype(BF16), preferred_element_type=F32) * qdec[...])
        kd = (k * kdec[...]).astype(BF16)
        st = st * cdec + lax.dot_general(kd, vb, TN_DIMS, preferred_element_type=F32)

        mu = jnp.mean(o, axis=-1, keepdims=True)
        oc = o - mu
        var = jnp.mean(oc * oc, axis=-1, keepdims=True)
        y = oc * lax.rsqrt(var + EPS) * gain_ref[...]
        o_ref[rs, :] = (_silu(g_ref[rs, :].astype(F32)) * y).astype(o_ref.dtype)
    state[...] = st


def retention_heads(proj, cos, sin, gain):
    s = proj.shape[0]
    c = RET_CHUNK
    log_gamma = np.log1p(-np.exp2(-5.0 - np.arange(RET_HEADS, dtype=np.float32))).astype(np.float32)
    col = lambda base: (lambda h, n, lg: (n, base + h))
    rows = min(c * RET_CHUNKS_PER_STEP, s)
    grid_spec = pltpu.PrefetchScalarGridSpec(
        num_scalar_prefetch=1,
        grid=(RET_HEADS, s // rows),
        in_specs=[pl.BlockSpec((rows, RET_D), col(0)),
                  pl.BlockSpec((rows, RET_D), col(RET_HEADS)),
                  pl.BlockSpec((rows, RET_D), col(2 * RET_HEADS)),
                  pl.BlockSpec((rows, RET_D), col(3 * RET_HEADS)),
                  pl.BlockSpec((rows, RET_D // 2), lambda h, n, lg: (n, 0)),
                  pl.BlockSpec((rows, RET_D // 2), lambda h, n, lg: (n, 0)),
                  pl.BlockSpec((1, RET_D), lambda h, n, lg: (0, h))],
        out_specs=pl.BlockSpec((rows, RET_D), lambda h, n, lg: (n, h)),
        scratch_shapes=[pltpu.VMEM((RET_D, RET_D), F32),
                        pltpu.VMEM((c, c), F32),
                        pltpu.VMEM((c, RET_D), F32),
                        pltpu.VMEM((c, RET_D), F32)],
    )
    return pl.pallas_call(
        _ret_kernel,
        grid_spec=grid_spec,
        out_shape=jax.ShapeDtypeStruct((s, RET_WIDTH), BF16),
        compiler_params=_params("arbitrary", "arbitrary"),
        name="retention",
    )(jnp.asarray(log_gamma), proj, proj, proj, proj, cos, sin, gain.reshape(1, RET_WIDTH))


def _split3(x):
    hi = x.astype(BF16)
    r1 = x - hi.astype(F32)
    mid = r1.astype(BF16)
    lo = (r1 - mid.astype(F32)).astype(BF16)
    return hi, mid, lo


def _gla_kernel(q_ref, k_ref, v_ref, g_ref, a_ref, wup_ref, b_ref, gain_ref, o_ref,
                state_t, pair_level, tri, attn_s):
    c = GLA_CHUNK

    @pl.when(pl.program_id(1) == 0)
    def _():
        state_t[...] = jnp.zeros_like(state_t)
        i = lax.broadcasted_iota(jnp.int32, (c, c), 0)
        j = lax.broadcasted_iota(jnp.int32, (c, c), 1)
        tri[...] = jnp.where(i >= j, 1.0, 0.0).astype(BF16)
        x = jnp.where(i > j, jnp.bitwise_xor(i, j), 0)
        lvl = jnp.full((c, c), GLA_LEVELS[-1] - 1, jnp.int32)
        for p in GLA_LEVELS:
            lvl = lvl + jnp.where(x >= (1 << p), 1, 0)
        pair_level[...] = lvl

    w_parts = _split3(wup_ref[...])
    st = state_t[...]
    for ci in range(GLA_CHUNKS_PER_STEP):
        cs = slice(ci * c, (ci + 1) * c)
        st = _gla_chunk(q_ref[cs, :], k_ref[cs, :], v_ref[cs, :], g_ref[cs, :], a_ref[cs, :],
                        w_parts, b_ref[...], gain_ref[...], o_ref.at[cs, :], st,
                        pair_level, tri, attn_s)
    state_t[...] = st


def _gla_chunk(q, k, v, g, a, w_parts, b, gain, o_ref, st, pair_level, tri, attn_s):
    c = GLA_CHUNK
    log2e = 1.4426950408889634
    a_hi, a_mid, _ = _split3(a)
    w_hi, w_mid, _ = w_parts
    dotf = lambda x, y: jnp.dot(x, y, preferred_element_type=F32)
    z = dotf(a_hi, w_hi) + (dotf(a_hi, w_mid) + dotf(a_mid, w_hi)) + b
    la = (jnp.minimum(z, 0.0) - jnp.log(1.0 + jnp.exp(-jnp.abs(z)))) * (log2e / GLA_GATE_TAU)
    l_hi, l_mid, l_lo = _split3(la)
    t = tri[...]
    cum = dotf(t, l_hi) + dotf(t, l_mid) + dotf(t, l_lo)

    qs = q.astype(F32) * (GLA_DK ** -0.5)
    k = k.astype(F32)
    vb = v.astype(BF16)

    o = lax.dot_general((qs * jnp.exp2(cum)).astype(BF16), st.astype(BF16), NT_DIMS,
                        preferred_element_type=F32)

    attn_s[...] = jnp.zeros_like(attn_s)
    for p in GLA_LEVELS:
        half = 1 << p
        nb = c // (2 * half)
        q_rows, k_rows = [], []
        for blk in range(nb):
            up = slice(blk * 2 * half, blk * 2 * half + half)
            lo = slice(blk * 2 * half + half, (blk + 1) * 2 * half)
            bnd = cum[up.stop - 1:up.stop, :]
            e_lo = jnp.exp2(cum[lo] - bnd)
            k_rows += [k[up] * jnp.exp2(bnd - cum[up]), k[lo] * e_lo]
            q_rows.append(qs[lo] * e_lo)
        q_t = q_rows[0] if nb == 1 else jnp.concatenate(q_rows, axis=0)
        k_t = jnp.concatenate(k_rows, axis=0)
        part = lax.dot_general(q_t.astype(BF16), k_t.astype(BF16), NT_DIMS,
                               preferred_element_type=F32)
        for blk in range(nb):
            lo = slice(blk * 2 * half + half, (blk + 1) * 2 * half)
            attn_s[lo, :] = jnp.where(pair_level[lo, :] == p,
                                      part[blk * half:(blk + 1) * half, :], attn_s[lo, :])

    rows = lax.broadcasted_iota(jnp.int32, (SUBLANES, 1), 0)
    lane = lax.broadcasted_iota(jnp.int32, (SUBLANES, LANES), 1)
    for blk in range(c // SUBLANES):
        r = blk * SUBLANES
        rs = slice(r, r + SUBLANES)
        ls = slice(r // LANES * LANES, r // LANES * LANES + LANES)
        q8, k8, c8 = qs[rs], k[rs], cum[rs]
        tile = attn_s[rs, ls]
        for m in range(SUBLANES):
            w = jnp.exp2(c8 - c8[m:m + 1, :])
            a = jnp.sum(q8 * w * k8[m:m + 1, :], axis=-1, keepdims=True)
            a = jnp.where(rows >= m, a, 0.0)
            tile = jnp.where(lane == r % LANES + m, a, tile)
        attn_s[rs, ls] = tile
    o = o + jnp.dot(attn_s[...].astype(BF16), vb, preferred_element_type=F32)

    last = cum[c - 1:c, :]
    kd = (k * jnp.exp2(last - cum)).astype(BF16)
    new_st = st * jnp.exp2(last) + lax.dot_general(vb, kd, TN_DIMS, preferred_element_type=F32)

    ms = jnp.mean(o * o, axis=-1, keepdims=True)
    y = o * lax.rsqrt(ms + EPS) * gain
    o_ref[...] = (_silu(g.astype(F32)) * y).astype(o_ref.dtype)
    return new_st


def gla_heads(proj, ga, w_up, b, gain):
    s = proj.shape[0]
    c = GLA_CHUNK
    qk_base = 4 * RET_WIDTH // GLA_DK
    v_base = (4 * RET_WIDTH + 2 * GLA_KEY_WIDTH) // GLA_DV
    w_up_pad = jnp.zeros((GATE_PAD, GLA_KEY_WIDTH), F32).at[:GLA_GATE_RANK].set(w_up)
    rows = c * GLA_CHUNKS_PER_STEP
    return pl.pallas_call(
        _gla_kernel,
        grid=(GLA_HEADS, s // rows),
        in_specs=[pl.BlockSpec((rows, GLA_DK), lambda h, n: (n, qk_base + h)),
                  pl.BlockSpec((rows, GLA_DK), lambda h, n: (n, qk_base + GLA_HEADS + h)),
                  pl.BlockSpec((rows, GLA_DV), lambda h, n: (n, v_base + h)),
                  pl.BlockSpec((rows, GLA_DV), lambda h, n: (n, v_base + GLA_HEADS + h)),
                  pl.BlockSpec((rows, GATE_PAD), lambda h, n: (n, 0)),
                  pl.BlockSpec((GATE_PAD, GLA_DK), lambda h, n: (0, h)),
                  pl.BlockSpec((1, GLA_DK), lambda h, n: (0, h)),
                  pl.BlockSpec((1, GLA_DV), lambda h, n: (0, h))],
        out_specs=pl.BlockSpec((rows, GLA_DV), lambda h, n: (n, h)),
        out_shape=jax.ShapeDtypeStruct((s, GLA_WIDTH), BF16),
        scratch_shapes=[pltpu.VMEM((GLA_DV, GLA_DK), F32),
                        pltpu.VMEM((c, c), jnp.int32),
                        pltpu.VMEM((c, c), BF16),
                        pltpu.VMEM((c, c), F32)],
        compiler_params=_params("arbitrary", "arbitrary"),
        name="gla",
    )(proj, proj, proj, proj, ga, w_up_pad, b.reshape(1, GLA_KEY_WIDTH), gain.reshape(1, GLA_WIDTH))


def kernel(x, positions, mix_norm, w_in, gla_w_up, gla_b, ret_gain, gla_gain, w_out, ffn_norm,
           w_gate, w_up, w_down, final_norm):
    b, s, d = x.shape
    x = x.reshape(b * s, d)
    cos, sin = rope_tables(positions.reshape(b * s))
    w_in_t = jnp.swapaxes(w_in, 1, 2)
    xg, r = norm_inputs(x, mix_norm[0])
    for l in range(DEPTH):
        proj, ga, wd = in_proj(xg, r, w_in_t, w_down, l)
        r_out = retention_heads(proj, cos, sin, ret_gain[l])
        g_out = gla_heads(proj, ga, gla_w_up[l], gla_b[l], gla_gain[l])
        x, xg, r = matmul_resid((r_out, g_out), w_out, x, layer=l, next_gain=ffn_norm[l],
                                tm=2048, single_buffer_a=True, row_splits=4, name="out_proj")
        hid = ffn_up(xg, r, w_gate, w_up, l)
        if l + 1 < DEPTH:
            x, xg, r = matmul_resid((hid,), wd, x, next_gain=mix_norm[l + 1], tm=512, name="ffn_down")
        else:
            x = matmul_resid((hid,), wd, x, tm=512, name="ffn_down")
    return rmsnorm(x, final_norm, F32).reshape(b, s, d)
```

```python
import functools

import numpy as np
import jax
import jax.numpy as jnp
from jax import lax
from jax.experimental import pallas as pl
from jax.experimental.pallas import tpu as pltpu

F32 = jnp.float32
BF16 = jnp.bfloat16

D_MODEL = 4096
DEPTH = 2
RET_HEADS = 8
RET_D = 256
RET_WIDTH = RET_HEADS * RET_D
GLA_HEADS = 4
GLA_DK = 256
GLA_DV = 512
GLA_KEY_WIDTH = GLA_HEADS * GLA_DK
GLA_WIDTH = GLA_HEADS * GLA_DV
GLA_GATE_RANK = 16
GLA_GATE_TAU = 16.0
FFN_HIDDEN = 11008
ROPE_BASE = 10000.0
EPS = 1e-6

MAIN_WIDTH = 4 * RET_WIDTH + 2 * GLA_KEY_WIDTH + 2 * GLA_WIDTH
LANES = 128
SUBLANES = 8
GATE_PAD = LANES

RET_CHUNK = 256
RET_CHUNKS_PER_STEP = 8
GLA_CHUNK = 256
GLA_CHUNKS_PER_STEP = 4
GLA_LEVELS = (7, 6, 5, 4, 3)

FFN_DOWN_TN = 512

VMEM_LIMIT = 62 * 1024 * 1024

NT_DIMS = (((1,), (1,)), ((), ()))
TN_DIMS = (((0,), (0,)), ((), ()))


def _params(*sem):
    return pltpu.CompilerParams(dimension_semantics=sem, vmem_limit_bytes=VMEM_LIMIT)


def _silu(x):
    return x * (1.0 / (1.0 + jnp.exp(-x)))


def _rmsnorm_kernel(x_ref, g_ref, o_ref):
    x = x_ref[...]
    ms = jnp.mean(x * x, axis=-1, keepdims=True)
    o_ref[...] = (x * lax.rsqrt(ms + EPS) * g_ref[...]).astype(o_ref.dtype)


def rmsnorm(x, gain, out_dtype, tm=256):
    s, d = x.shape
    return pl.pallas_call(
        _rmsnorm_kernel,
        grid=(s // tm,),
        in_specs=[pl.BlockSpec((tm, d), lambda i: (i, 0)),
                  pl.BlockSpec((1, d), lambda i: (0, 0))],
        out_specs=pl.BlockSpec((tm, d), lambda i: (i, 0)),
        out_shape=jax.ShapeDtypeStruct((s, d), out_dtype),
        compiler_params=_params("arbitrary"),
        name="rmsnorm",
    )(x, gain.reshape(1, d))


def _fold_lanes(v):
    acc = v[:, :LANES]
    for g in range(1, v.shape[1] // LANES):
        acc = acc + v[:, g * LANES:(g + 1) * LANES]
    return acc


def _row_scale(ss):
    r = lax.rsqrt(jnp.sum(ss, axis=-1, keepdims=True) * (1.0 / D_MODEL) + EPS)
    return jnp.broadcast_to(r, ss.shape)


def _scale_rows(acc, r):
    return acc * jnp.concatenate([r] * (acc.shape[1] // LANES), axis=1)


def _norm_inputs_kernel(x_ref, g_ref, xg_ref, r_ref):
    x = x_ref[...]
    xg_ref[...] = (x * g_ref[...]).astype(BF16)
    r_ref[...] = _row_scale(_fold_lanes(x * x))


def norm_inputs(x, gain, tm=256):
    s, d = x.shape
    return pl.pallas_call(
        _norm_inputs_kernel,
        grid=(s // tm,),
        in_specs=[pl.BlockSpec((tm, d), lambda i: (i, 0)),
                  pl.BlockSpec((1, d), lambda i: (0, 0))],
        out_specs=[pl.BlockSpec((tm, d), lambda i: (i, 0)),
                   pl.BlockSpec((tm, LANES), lambda i: (i, 0))],
        out_shape=[jax.ShapeDtypeStruct((s, d), BF16),
                   jax.ShapeDtypeStruct((s, LANES), F32)],
        compiler_params=_params("arbitrary"),
        name="norm_inputs",
    )(x, gain.reshape(1, d))


def _row_block_spec(tm, k, single_buffer=True):
    return pl.BlockSpec((tm, k), lambda i, j: (i, 0),
                        pipeline_mode=pl.Buffered(1) if single_buffer else None)


def _mm_resid_kernel(*all_refs, n_parts, row_splits):
    operand_refs, res_ref, refs = all_refs[:2 * n_parts], all_refs[2 * n_parts], all_refs[2 * n_parts + 1:]
    emit_norm = len(refs) > 1
    if emit_norm:
        g_ref, o_ref, xg_ref, r_ref = refs
        j = pl.program_id(1)

        @pl.when(j == 0)
        def _():
            r_ref[...] = jnp.zeros_like(r_ref)
    else:
        o_ref, = refs

    ws = [operand_refs[2 * p + 1][...].astype(BF16) for p in range(n_parts)]
    rows = o_ref.shape[0] // row_splits
    for s in range(row_splits):
        rs = slice(s * rows, (s + 1) * rows)
        x = res_ref[rs, :]
        for p in range(n_parts):
            x = x + jnp.dot(operand_refs[2 * p][rs, :], ws[p], preferred_element_type=F32)
        o_ref[rs, :] = x
        if emit_norm:
            xg_ref[rs, :] = (x * g_ref[...]).astype(BF16)
            r_ref[rs, :] += _fold_lanes(x * x)

    if emit_norm:
        @pl.when(j == pl.num_programs(1) - 1)
        def _():
            r_ref[...] = _row_scale(r_ref[...])


def matmul_resid(a_parts, w, resid, layer=None, next_gain=None, tm=1024, tn=512,
                 single_buffer_a=False, row_splits=1, name="matmul"):
    m, k = a_parts[0].shape
    n = w.shape[-1] if layer is not None else w.shape[0] * w.shape[2]
    assert layer is not None or w.shape[2] == tn
    tile = pl.BlockSpec((tm, tn), lambda i, j: (i, j))
    in_specs, args = [], []
    for p, a in enumerate(a_parts):
        in_specs.append(_row_block_spec(tm, k, single_buffer_a))
        if layer is not None:
            in_specs.append(pl.BlockSpec((None, k, tn), lambda i, j, p=p: (layer, p, j)))
        else:
            in_specs.append(pl.BlockSpec((None, k, tn), lambda i, j, p=p: (j, p, 0)))
        args += [a, w]
    in_specs.append(tile)
    args.append(resid)
    out_specs, out_shape = [tile], [jax.ShapeDtypeStruct((m, n), F32)]
    if next_gain is not None:
        in_specs.append(pl.BlockSpec((1, tn), lambda i, j: (0, j)))
        args.append(next_gain.reshape(1, n))
        out_specs += [tile, pl.BlockSpec((tm, LANES), lambda i, j: (i, 0))]
        out_shape += [jax.ShapeDtypeStruct((m, n), BF16), jax.ShapeDtypeStruct((m, LANES), F32)]
    outs = pl.pallas_call(
        functools.partial(_mm_resid_kernel, n_parts=len(a_parts), row_splits=row_splits),
        grid=(m // tm, n // tn),
        in_specs=in_specs,
        out_specs=out_specs,
        out_shape=out_shape,
        compiler_params=_params("arbitrary", "arbitrary"),
        name=name,
    )(*args)
    return outs if next_gain is not None else outs[0]


def _in_proj_kernel(xg_ref, r_ref, w_ref, wlr_ref, wd_ref, o_ref, ga_ref, wd_bf16_ref):
    acc = lax.dot_general(xg_ref[...], w_ref[...].astype(BF16), NT_DIMS, preferred_element_type=F32)
    o_ref[...] = _scale_rows(acc, r_ref[...]).astype(o_ref.dtype)
    tn_down = wd_bf16_ref.shape[2]
    for t in range(wd_bf16_ref.shape[0]):
        wd_bf16_ref[t] = wd_ref[:, t * tn_down:(t + 1) * tn_down].astype(BF16)

    @pl.when(pl.program_id(1) == 0)
    def _():
        wlr = wlr_ref[...]
        row = lax.broadcasted_iota(jnp.int32, wlr.shape, 0)
        wlr = jnp.where(row < GLA_GATE_RANK, wlr, 0.0)
        ga = lax.dot_general(xg_ref[...], wlr.astype(BF16), NT_DIMS, preferred_element_type=F32)
        ga_ref[...] = ga * r_ref[...]


def in_proj(xg, r, w_in_t, w_down, layer, tm=2048, tn=512, tr=128, tn_down=FFN_DOWN_TN):
    m, k = xg.shape
    _, f, d = w_down.shape
    nj = MAIN_WIDTH // tn
    n_wd = f // tr
    assert n_wd * tr == f and n_wd <= (m // tm) * nj and d % tn_down == 0
    wd_block = lambda i, j: jnp.minimum(i * nj + j, n_wd - 1)
    return pl.pallas_call(
        _in_proj_kernel,
        grid=(m // tm, nj),
        in_specs=[_row_block_spec(tm, k),
                  _row_block_spec(tm, LANES),
                  pl.BlockSpec((None, tn, k), lambda i, j: (layer, j, 0)),
                  pl.BlockSpec((None, GATE_PAD, k), lambda i, j: (layer, MAIN_WIDTH // GATE_PAD, 0)),
                  pl.BlockSpec((None, tr, d), lambda i, j: (layer, wd_block(i, j), 0))],
        out_specs=[pl.BlockSpec((tm, tn), lambda i, j: (i, j)),
                   pl.BlockSpec((tm, GATE_PAD), lambda i, j: (i, 0)),
                   pl.BlockSpec((d // tn_down, tr, tn_down), lambda i, j: (0, wd_block(i, j), 0))],
        out_shape=[jax.ShapeDtypeStruct((m, MAIN_WIDTH), BF16),
                   jax.ShapeDtypeStruct((m, GATE_PAD), F32),
                   jax.ShapeDtypeStruct((d // tn_down, f, tn_down), BF16)],
        compiler_params=_params("arbitrary", "arbitrary"),
        name="in_proj",
    )(xg, r, w_in_t, w_in_t, w_down)


def _ffn_up_kernel(xg_ref, r_ref, wg_ref, wu_ref, o_ref, *, row_splits):
    wg = wg_ref[...].astype(BF16)
    wu = wu_ref[...].astype(BF16)
    rows = xg_ref.shape[0] // row_splits
    for s in range(row_splits):
        rs = slice(s * rows, (s + 1) * rows)
        g = _scale_rows(jnp.dot(xg_ref[rs, :], wg, preferred_element_type=F32), r_ref[rs, :])
        u = _scale_rows(jnp.dot(xg_ref[rs, :], wu, preferred_element_type=F32), r_ref[rs, :])
        o_ref[rs, :] = (_silu(g) * u).astype(o_ref.dtype)


def ffn_up(xg, r, w_gate, w_up, layer, tm=4096, tn=256, row_splits=4):
    m, k = xg.shape
    n = w_gate.shape[2]
    w_spec = pl.BlockSpec((None, k, tn), lambda i, j: (layer, 0, j))
    return pl.pallas_call(
        functools.partial(_ffn_up_kernel, row_splits=row_splits),
        grid=(m // tm, n // tn),
        in_specs=[_row_block_spec(tm, k), _row_block_spec(tm, LANES), w_spec, w_spec],
        out_specs=pl.BlockSpec((tm, tn), lambda i, j: (i, j)),
        out_shape=jax.ShapeDtypeStruct((m, n), BF16),
        compiler_params=_params("arbitrary", "arbitrary"),
        name="ffn_up",
    )(xg, r, w_gate, w_up)


def _rope_kernel(pos_ref, inv_ref, cos_ref, sin_ref):
    ang = pos_ref[...].astype(F32) * inv_ref[...]
    cos_ref[...] = jnp.cos(ang)
    sin_ref[...] = jnp.sin(ang)


def rope_tables(positions, tm=1024):
    s = positions.shape[0]
    tm = min(tm, s)
    half = RET_D // 2
    inv_freq = (ROPE_BASE ** (-np.arange(half, dtype=np.float32) / half)).astype(np.float32)
    return pl.pallas_call(
        _rope_kernel,
        grid=(s // tm,),
        in_specs=[pl.BlockSpec((tm, 1), lambda i: (i, 0)),
                  pl.BlockSpec((1, half), lambda i: (0, 0))],
        out_specs=[pl.BlockSpec((tm, half), lambda i: (i, 0))] * 2,
        out_shape=[jax.ShapeDtypeStruct((s, half), F32)] * 2,
        compiler_params=_params("arbitrary"),
        name="rope_tables",
    )(positions.reshape(s, 1), jnp.asarray(inv_freq).reshape(1, half))


def _ret_kernel(lg_ref, q_ref, k_ref, v_ref, g_ref, cos_ref, sin_ref, gain_ref, o_ref,
                state, decay, qdec, kdec):
    c = RET_CHUNK
    lg = lg_ref[pl.program_id(0)]

    @pl.when(pl.program_id(1) == 0)
    def _():
        state[...] = jnp.zeros_like(state)
        i = lax.broadcasted_iota(jnp.int32, (c, c), 0)
        j = lax.broadcasted_iota(jnp.int32, (c, c), 1)
        diff = (i - j).astype(F32)
        k_scale = RET_D ** -0.5
        decay[...] = jnp.where(diff >= 0, jnp.exp(lg * jnp.maximum(diff, 0.0)) * k_scale, 0.0)
        r = lax.broadcasted_iota(jnp.int32, (c, RET_D), 0).astype(F32)
        qdec[...] = jnp.exp(lg * (r + 1.0))
        kdec[...] = jnp.exp(lg * (c - 1.0 - r)) * k_scale

    half = RET_D // 2
    cdec = jnp.exp(jnp.zeros((1, RET_D), F32) + lg * c)
    st = state[...]
    for ci in range(RET_CHUNKS_PER_STEP):
        rs = slice(ci * c, (ci + 1) * c)
        cos = cos_ref[rs, :]
        sin = sin_ref[rs, :]

        def rot(t):
            t1, t2 = t[:, :half], t[:, half:]
            return jnp.concatenate([t1 * cos - t2 * sin, t1 * sin + t2 * cos], axis=1)

        q = rot(q_ref[rs, :].astype(F32))
        k = rot(k_ref[rs, :].astype(F32))
        vb = v_ref[rs, :].astype(BF16)
        qb = q.astype(BF16)
        scores = lax.dot_general(qb, k.astype(BF16), NT_DIMS, preferred_element_type=F32) * decay[...]
        o = (jnp.dot(scores.astype(BF16), vb, preferred_element_type=F32)
             + jnp.dot(qb, st.astype(BF16), preferred_element_type=F32) * qdec[...])
        kd = (k * kdec[...]).astype(BF16)
        st = st * cdec + lax.dot_general(kd, vb, TN_DIMS, preferred_element_type=F32)

        mu = jnp.mean(o, axis=-1, keepdims=True)
        oc = o - mu
        var = jnp.mean(oc * oc, axis=-1, keepdims=True)
        y = oc * lax.rsqrt(var + EPS) * gain_ref[...]
        o_ref[rs, :] = (_silu(g_ref[rs, :].astype(F32)) * y).astype(o_ref.dtype)
    state[...] = st


def retention_heads(proj, cos, sin, gain):
    s = proj.shape[0]
    c = RET_CHUNK
    log_gamma = np.log1p(-np.exp2(-5.0 - np.arange(RET_HEADS, dtype=np.float32))).astype(np.float32)
    col = lambda base: (lambda h, n, lg: (n, base + h))
    rows = min(c * RET_CHUNKS_PER_STEP, s)
    grid_spec = pltpu.PrefetchScalarGridSpec(
        num_scalar_prefetch=1,
        grid=(RET_HEADS, s // rows),
        in_specs=[pl.BlockSpec((rows, RET_D), col(0)),
                  pl.BlockSpec((rows, RET_D), col(RET_HEADS)),
                  pl.BlockSpec((rows, RET_D), col(2 * RET_HEADS)),
                  pl.BlockSpec((rows, RET_D), col(3 * RET_HEADS)),
                  pl.BlockSpec((rows, RET_D // 2), lambda h, n, lg: (n, 0)),
                  pl.BlockSpec((rows, RET_D // 2), lambda h, n, lg: (n, 0)),
                  pl.BlockSpec((1, RET_D), lambda h, n, lg: (0, h))],
        out_specs=pl.BlockSpec((rows, RET_D), lambda h, n, lg: (n, h)),
        scratch_shapes=[pltpu.VMEM((RET_D, RET_D), F32),
                        pltpu.VMEM((c, c), F32),
                        pltpu.VMEM((c, RET_D), F32),
                        pltpu.VMEM((c, RET_D), F32)],
    )
    return pl.pallas_call(
        _ret_kernel,
        grid_spec=grid_spec,
        out_shape=jax.ShapeDtypeStruct((s, RET_WIDTH), BF16),
        compiler_params=_params("arbitrary", "arbitrary"),
        name="retention",
    )(jnp.asarray(log_gamma), proj, proj, proj, proj, cos, sin, gain.reshape(1, RET_WIDTH))


def _split3(x):
    hi = x.astype(BF16)
    r1 = x - hi.astype(F32)
    mid = r1.astype(BF16)
    lo = (r1 - mid.astype(F32)).astype(BF16)
    return hi, mid, lo


def _gla_kernel(q_ref, k_ref, v_ref, g_ref, a_ref, wup_ref, b_ref, gain_ref, o_ref,
                state_t, pair_level, tri, attn_s):
    c = GLA_CHUNK

    @pl.when(pl.program_id(1) == 0)
    def _():
        state_t[...] = jnp.zeros_like(state_t)
        i = lax.broadcasted_iota(jnp.int32, (c, c), 0)
        j = lax.broadcasted_iota(jnp.int32, (c, c), 1)
        tri[...] = jnp.where(i >= j, 1.0, 0.0).astype(BF16)
        x = jnp.where(i > j, jnp.bitwise_xor(i, j), 0)
        lvl = jnp.full((c, c), GLA_LEVELS[-1] - 1, jnp.int32)
        for p in GLA_LEVELS:
            lvl = lvl + jnp.where(x >= (1 << p), 1, 0)
        pair_level[...] = lvl

    w_parts = _split3(wup_ref[...])
    st = state_t[...]
    for ci in range(GLA_CHUNKS_PER_STEP):
        cs = slice(ci * c, (ci + 1) * c)
        st = _gla_chunk(q_ref[cs, :], k_ref[cs, :], v_ref[cs, :], g_ref[cs, :], a_ref[cs, :],
                        w_parts, b_ref[...], gain_ref[...], o_ref.at[cs, :], st,
                        pair_level, tri, attn_s)
    state_t[...] = st


def _gla_chunk(q, k, v, g, a, w_parts, b, gain, o_ref, st, pair_level, tri, attn_s):
    c = GLA_CHUNK
    log2e = 1.4426950408889634
    a_hi, a_mid, _ = _split3(a)
    w_hi, w_mid, _ = w_parts
    dotf = lambda x, y: jnp.dot(x, y, preferred_element_type=F32)
    z = dotf(a_hi, w_hi) + (dotf(a_hi, w_mid) + dotf(a_mid, w_hi)) + b
    la = (jnp.minimum(z, 0.0) - jnp.log(1.0 + jnp.exp(-jnp.abs(z)))) * (log2e / GLA_GATE_TAU)
    l_hi, l_mid, l_lo = _split3(la)
    t = tri[...]
    cum = dotf(t, l_hi) + dotf(t, l_mid) + dotf(t, l_lo)

    qs = q.astype(F32) * (GLA_DK ** -0.5)
    k = k.astype(F32)
    vb = v.astype(BF16)

    o = lax.dot_general((qs * jnp.exp2(cum)).astype(BF16), st.astype(BF16), NT_DIMS,
                        preferred_element_type=F32)

    attn_s[...] = jnp.zeros_like(attn_s)
    for p in GLA_LEVELS:
        half = 1 << p
        nb = c // (2 * half)
        q_rows, k_rows = [], []
        for blk in range(nb):
            up = slice(blk * 2 * half, blk * 2 * half + half)
            lo = slice(blk * 2 * half + half, (blk + 1) * 2 * half)
            bnd = cum[up.stop - 1:up.stop, :]
            e_lo = jnp.exp2(cum[lo] - bnd)
            k_rows += [k[up] * jnp.exp2(bnd - cum[up]), k[lo] * e_lo]
            q_rows.append(qs[lo] * e_lo)
        q_t = q_rows[0] if nb == 1 else jnp.concatenate(q_rows, axis=0)
        k_t = jnp.concatenate(k_rows, axis=0)
        part = lax.dot_general(q_t.astype(BF16), k_t.astype(BF16), NT_DIMS,
                               preferred_element_type=F32)
        for blk in range(nb):
            lo = slice(blk * 2 * half + half, (blk + 1) * 2 * half)
            attn_s[lo, :] = jnp.where(pair_level[lo, :] == p,
                                      part[blk * half:(blk + 1) * half, :], attn_s[lo, :])

    rows = lax.broadcasted_iota(jnp.int32, (SUBLANES, 1), 0)
    lane = lax.broadcasted_iota(jnp.int32, (SUBLANES, LANES), 1)
    for blk in range(c // SUBLANES):
        r = blk * SUBLANES
        rs = slice(r, r + SUBLANES)
        ls = slice(r // LANES * LANES, r // LANES * LANES + LANES)
        q8, k8, c8 = qs[rs], k[rs], cum[rs]
        tile = attn_s[rs, ls]
        for m in range(SUBLANES):
            w = jnp.exp2(c8 - c8[m:m + 1, :])
            a = jnp.sum(q8 * w * k8[m:m + 1, :], axis=-1, keepdims=True)
            a = jnp.where(rows >= m, a, 0.0)
            tile = jnp.where(lane == r % LANES + m, a, tile)
        attn_s[rs, ls] = tile
    o = o + jnp.dot(attn_s[...].astype(BF16), vb, preferred_element_type=F32)

    last = cum[c - 1:c, :]
    kd = (k * jnp.exp2(last - cum)).astype(BF16)
    new_st = st * jnp.exp2(last) + lax.dot_general(vb, kd, TN_DIMS, preferred_element_type=F32)

    ms = jnp.mean(o * o, axis=-1, keepdims=True)
    y = o * lax.rsqrt(ms + EPS) * gain
    o_ref[...] = (_silu(g.astype(F32)) * y).astype(o_ref.dtype)
    return new_st


def gla_heads(proj, ga, w_up, b, gain):
    s = proj.shape[0]
    c = GLA_CHUNK
    qk_base = 4 * RET_WIDTH // GLA_DK
    v_base = (4 * RET_WIDTH + 2 * GLA_KEY_WIDTH) // GLA_DV
    w_up_pad = jnp.zeros((GATE_PAD, GLA_KEY_WIDTH), F32).at[:GLA_GATE_RANK].set(w_up)
    rows = c * GLA_CHUNKS_PER_STEP
    return pl.pallas_call(
        _gla_kernel,
        grid=(GLA_HEADS, s // rows),
        in_specs=[pl.BlockSpec((rows, GLA_DK), lambda h, n: (n, qk_base + h)),
                  pl.BlockSpec((rows, GLA_DK), lambda h, n: (n, qk_base + GLA_HEADS + h)),
                  pl.BlockSpec((rows, GLA_DV), lambda h, n: (n, v_base + h)),
                  pl.BlockSpec((rows, GLA_DV), lambda h, n: (n, v_base + GLA_HEADS + h)),
                  pl.BlockSpec((rows, GATE_PAD), lambda h, n: (n, 0)),
                  pl.BlockSpec((GATE_PAD, GLA_DK), lambda h, n: (0, h)),
                  pl.BlockSpec((1, GLA_DK), lambda h, n: (0, h)),
                  pl.BlockSpec((1, GLA_DV), lambda h, n: (0, h))],
        out_specs=pl.BlockSpec((rows, GLA_DV), lambda h, n: (n, h)),
        out_shape=jax.ShapeDtypeStruct((s, GLA_WIDTH), BF16),
        scratch_shapes=[pltpu.VMEM((GLA_DV, GLA_DK), F32),
                        pltpu.VMEM((c, c), jnp.int32),
                        pltpu.VMEM((c, c), BF16),
                        pltpu.VMEM((c, c), F32)],
        compiler_params=_params("arbitrary", "arbitrary"),
        name="gla",
    )(proj, proj, proj, proj, ga, w_up_pad, b.reshape(1, GLA_KEY_WIDTH), gain.reshape(1, GLA_WIDTH))


def kernel(x, positions, mix_norm, w_in, gla_w_up, gla_b, ret_gain, gla_gain, w_out, ffn_norm,
           w_gate, w_up, w_down, final_norm):
    b, s, d = x.shape
    x = x.reshape(b * s, d)
    cos, sin = rope_tables(positions.reshape(b * s))
    w_in_t = jnp.swapaxes(w_in, 1, 2)
    xg, r = norm_inputs(x, mix_norm[0])
    for l in range(DEPTH):
        proj, ga, wd = in_proj(xg, r, w_in_t, w_down, l)
        r_out = retention_heads(proj, cos, sin, ret_gain[l])
        g_out = gla_heads(proj, ga, gla_w_up[l], gla_b[l], gla_gain[l])
        x, xg, r = matmul_resid((r_out, g_out), w_out, x, layer=l, next_gain=ffn_norm[l],
                                tm=2048, single_buffer_a=True, row_splits=4, name="out_proj")
        hid = ffn_up(xg, r, w_gate, w_up, l)
        if l + 1 < DEPTH:
            x, xg, r = matmul_resid((hid,), wd, x, next_gain=mix_norm[l + 1], tm=512, name="ffn_down")
        else:
            x = matmul_resid((hid,), wd, x, tm=512, name="ffn_down")
    return rmsnorm(x, final_norm, F32).reshape(b, s, d)
```

```python
import functools

import numpy as np
import jax
import jax.numpy as jnp
from jax import lax
from jax.experimental import pallas as pl
from jax.experimental.pallas import tpu as pltpu

F32 = jnp.float32
BF16 = jnp.bfloat16

D_MODEL = 4096
DEPTH = 2
RET_HEADS = 8
RET_D = 256
RET_WIDTH = RET_HEADS * RET_D
GLA_HEADS = 4
GLA_DK = 256
GLA_DV = 512
GLA_KEY_WIDTH = GLA_HEADS * GLA_DK
GLA_WIDTH = GLA_HEADS * GLA_DV
GLA_GATE_RANK = 16
GLA_GATE_TAU = 16.0
FFN_HIDDEN = 11008
ROPE_BASE = 10000.0
EPS = 1e-6

MAIN_WIDTH = 4 * RET_WIDTH + 2 * GLA_KEY_WIDTH + 2 * GLA_WIDTH
LANES = 128
SUBLANES = 8
GATE_PAD = LANES

RET_CHUNK = 256
RET_CHUNKS_PER_STEP = 16
GLA_CHUNK = 256
GLA_CHUNKS_PER_STEP = 8
GLA_LEVELS = (7, 6, 5, 4, 3)

VMEM_LIMIT = 62 * 1024 * 1024

NT_DIMS = (((1,), (1,)), ((), ()))
TN_DIMS = (((0,), (0,)), ((), ()))


def _params(*sem):
    return pltpu.CompilerParams(dimension_semantics=sem, vmem_limit_bytes=VMEM_LIMIT)


def _silu(x):
    return x * (1.0 / (1.0 + jnp.exp(-x)))


def _rmsnorm_kernel(x_ref, g_ref, o_ref):
    x = x_ref[...]
    ms = jnp.mean(x * x, axis=-1, keepdims=True)
    o_ref[...] = (x * lax.rsqrt(ms + EPS) * g_ref[...]).astype(o_ref.dtype)


def rmsnorm(x, gain, out_dtype, tm=256):
    s, d = x.shape
    return pl.pallas_call(
        _rmsnorm_kernel,
        grid=(s // tm,),
        in_specs=[pl.BlockSpec((tm, d), lambda i: (i, 0)),
                  pl.BlockSpec((1, d), lambda i: (0, 0))],
        out_specs=pl.BlockSpec((tm, d), lambda i: (i, 0)),
        out_shape=jax.ShapeDtypeStruct((s, d), out_dtype),
        compiler_params=_params("arbitrary"),
        name="rmsnorm",
    )(x, gain.reshape(1, d))


def _fold_lanes(v):
    acc = v[:, :LANES]
    for g in range(1, v.shape[1] // LANES):
        acc = acc + v[:, g * LANES:(g + 1) * LANES]
    return acc


def _row_scale(ss):
    r = lax.rsqrt(jnp.sum(ss, axis=-1, keepdims=True) * (1.0 / D_MODEL) + EPS)
    return jnp.broadcast_to(r, ss.shape)


def _scale_rows(acc, r):
    return acc * jnp.concatenate([r] * (acc.shape[1] // LANES), axis=1)


def _norm_inputs_kernel(x_ref, g_ref, xg_ref, r_ref):
    x = x_ref[...]
    xg_ref[...] = (x * g_ref[...]).astype(BF16)
    r_ref[...] = _row_scale(_fold_lanes(x * x))


def norm_inputs(x, gain, tm=256):
    s, d = x.shape
    return pl.pallas_call(
        _norm_inputs_kernel,
        grid=(s // tm,),
        in_specs=[pl.BlockSpec((tm, d), lambda i: (i, 0)),
                  pl.BlockSpec((1, d), lambda i: (0, 0))],
        out_specs=[pl.BlockSpec((tm, d), lambda i: (i, 0)),
                   pl.BlockSpec((tm, LANES), lambda i: (i, 0))],
        out_shape=[jax.ShapeDtypeStruct((s, d), BF16),
                   jax.ShapeDtypeStruct((s, LANES), F32)],
        compiler_params=_params("arbitrary"),
        name="norm_inputs",
    )(x, gain.reshape(1, d))


def _row_block_spec(tm, k, single_buffer=True):
    return pl.BlockSpec((tm, k), lambda i, j: (i, 0),
                        pipeline_mode=pl.Buffered(1) if single_buffer else None)


def _mm_resid_kernel(*all_refs, n_parts, row_splits):
    operand_refs, res_ref, refs = all_refs[:2 * n_parts], all_refs[2 * n_parts], all_refs[2 * n_parts + 1:]
    emit_norm = len(refs) > 1
    if emit_norm:
        g_ref, o_ref, xg_ref, r_ref = refs
        j = pl.program_id(1)

        @pl.when(j == 0)
        def _():
            r_ref[...] = jnp.zeros_like(r_ref)
    else:
        o_ref, = refs

    ws = [operand_refs[2 * p + 1][...].astype(BF16) for p in range(n_parts)]
    rows = o_ref.shape[0] // row_splits
    for s in range(row_splits):
        rs = slice(s * rows, (s + 1) * rows)
        x = res_ref[rs, :]
        for p in range(n_parts):
            x = x + jnp.dot(operand_refs[2 * p][rs, :], ws[p], preferred_element_type=F32)
        o_ref[rs, :] = x
        if emit_norm:
            xg_ref[rs, :] = (x * g_ref[...]).astype(BF16)
            r_ref[rs, :] += _fold_lanes(x * x)

    if emit_norm:
        @pl.when(j == pl.num_programs(1) - 1)
        def _():
            r_ref[...] = _row_scale(r_ref[...])


def matmul_resid(a_parts, w, resid, layer=None, next_gain=None, tm=1024, tn=512,
                 single_buffer_a=False, row_splits=1, name="matmul"):
    m, k = a_parts[0].shape
    n = w.shape[-1]
    tile = pl.BlockSpec((tm, tn), lambda i, j: (i, j))
    in_specs, args = [], []
    for p, a in enumerate(a_parts):
        in_specs.append(_row_block_spec(tm, k, single_buffer_a))
        if w.ndim == 3:
            in_specs.append(pl.BlockSpec((None, k, tn), lambda i, j, p=p: (layer, p, j)))
        else:
            in_specs.append(pl.BlockSpec((k, tn), lambda i, j, p=p: (p, j)))
        args += [a, w]
    in_specs.append(tile)
    args.append(resid)
    out_specs, out_shape = [tile], [jax.ShapeDtypeStruct((m, n), F32)]
    if next_gain is not None:
        in_specs.append(pl.BlockSpec((1, tn), lambda i, j: (0, j)))
        args.append(next_gain.reshape(1, n))
        out_specs += [tile, pl.BlockSpec((tm, LANES), lambda i, j: (i, 0))]
        out_shape += [jax.ShapeDtypeStruct((m, n), BF16), jax.ShapeDtypeStruct((m, LANES), F32)]
    outs = pl.pallas_call(
        functools.partial(_mm_resid_kernel, n_parts=len(a_parts), row_splits=row_splits),
        grid=(m // tm, n // tn),
        in_specs=in_specs,
        out_specs=out_specs,
        out_shape=out_shape,
        compiler_params=_params("arbitrary", "arbitrary"),
        name=name,
    )(*args)
    return outs if next_gain is not None else outs[0]


def _in_proj_kernel(xg_ref, r_ref, w_ref, wlr_ref, wd_ref, o_ref, ga_ref, wd_bf16_ref):
    acc = lax.dot_general(xg_ref[...], w_ref[...].astype(BF16), NT_DIMS, preferred_element_type=F32)
    o_ref[...] = _scale_rows(acc, r_ref[...]).astype(o_ref.dtype)
    wd_bf16_ref[...] = wd_ref[...].astype(BF16)

    @pl.when(pl.program_id(1) == 0)
    def _():
        wlr = wlr_ref[...]
        row = lax.broadcasted_iota(jnp.int32, wlr.shape, 0)
        wlr = jnp.where(row < GLA_GATE_RANK, wlr, 0.0)
        ga = lax.dot_general(xg_ref[...], wlr.astype(BF16), NT_DIMS, preferred_element_type=F32)
        ga_ref[...] = ga * r_ref[...]


def in_proj(xg, r, w_in_t, w_down, layer, tm=2048, tn=512, tr=128):
    m, k = xg.shape
    _, f, d = w_down.shape
    nj = MAIN_WIDTH // tn
    n_wd = f // tr
    assert n_wd * tr == f and n_wd <= (m // tm) * nj
    wd_block = lambda i, j: jnp.minimum(i * nj + j, n_wd - 1)
    return pl.pallas_call(
        _in_proj_kernel,
        grid=(m // tm, nj),
        in_specs=[_row_block_spec(tm, k),
                  _row_block_spec(tm, LANES),
                  pl.BlockSpec((None, tn, k), lambda i, j: (layer, j, 0)),
                  pl.BlockSpec((None, GATE_PAD, k), lambda i, j: (layer, MAIN_WIDTH // GATE_PAD, 0)),
                  pl.BlockSpec((None, tr, d), lambda i, j: (layer, wd_block(i, j), 0))],
        out_specs=[pl.BlockSpec((tm, tn), lambda i, j: (i, j)),
                   pl.BlockSpec((tm, GATE_PAD), lambda i, j: (i, 0)),
                   pl.BlockSpec((tr, d), lambda i, j: (wd_block(i, j), 0))],
        out_shape=[jax.ShapeDtypeStruct((m, MAIN_WIDTH), BF16),
                   jax.ShapeDtypeStruct((m, GATE_PAD), F32),
                   jax.ShapeDtypeStruct((f, d), BF16)],
        compiler_params=_params("arbitrary", "arbitrary"),
        name="in_proj",
    )(xg, r, w_in_t, w_in_t, w_down)


def _ffn_up_kernel(xg_ref, r_ref, wg_ref, wu_ref, o_ref, *, row_splits):
    wg = wg_ref[...].astype(BF16)
    wu = wu_ref[...].astype(BF16)
    rows = xg_ref.shape[0] // row_splits
    for s in range(row_splits):
        rs = slice(s * rows, (s + 1) * rows)
        g = _scale_rows(jnp.dot(xg_ref[rs, :], wg, preferred_element_type=F32), r_ref[rs, :])
        u = _scale_rows(jnp.dot(xg_ref[rs, :], wu, preferred_element_type=F32), r_ref[rs, :])
        o_ref[rs, :] = (_silu(g) * u).astype(o_ref.dtype)


def ffn_up(xg, r, w_gate, w_up, layer, tm=4096, tn=256, row_splits=4):
    m, k = xg.shape
    n = w_gate.shape[2]
    w_spec = pl.BlockSpec((None, k, tn), lambda i, j: (layer, 0, j))
    return pl.pallas_call(
        functools.partial(_ffn_up_kernel, row_splits=row_splits),
        grid=(m // tm, n // tn),
        in_specs=[_row_block_spec(tm, k), _row_block_spec(tm, LANES), w_spec, w_spec],
        out_specs=pl.BlockSpec((tm, tn), lambda i, j: (i, j)),
        out_shape=jax.ShapeDtypeStruct((m, n), BF16),
        compiler_params=_params("arbitrary", "arbitrary"),
        name="ffn_up",
    )(xg, r, w_gate, w_up)


def _rope_kernel(pos_ref, inv_ref, cos_ref, sin_ref):
    ang = pos_ref[...].astype(F32) * inv_ref[...]
    cos_ref[...] = jnp.cos(ang)
    sin_ref[...] = jnp.sin(ang)


def rope_tables(positions, tm=1024):
    s = positions.shape[0]
    tm = min(tm, s)
    half = RET_D // 2
    inv_freq = (ROPE_BASE ** (-np.arange(half, dtype=np.float32) / half)).astype(np.float32)
    return pl.pallas_call(
        _rope_kernel,
        grid=(s // tm,),
        in_specs=[pl.BlockSpec((tm, 1), lambda i: (i, 0)),
                  pl.BlockSpec((1, half), lambda i: (0, 0))],
        out_specs=[pl.BlockSpec((tm, half), lambda i: (i, 0))] * 2,
        out_shape=[jax.ShapeDtypeStruct((s, half), F32)] * 2,
        compiler_params=_params("arbitrary"),
        name="rope_tables",
    )(positions.reshape(s, 1), jnp.asarray(inv_freq).reshape(1, half))


def _ret_kernel(lg_ref, q_ref, k_ref, v_ref, g_ref, cos_ref, sin_ref, gain_ref, o_ref,
                state, decay, qdec, kdec):
    c = RET_CHUNK
    lg = lg_ref[pl.program_id(0)]

    @pl.when(pl.program_id(1) == 0)
    def _():
        state[...] = jnp.zeros_like(state)
        i = lax.broadcasted_iota(jnp.int32, (c, c), 0)
        j = lax.broadcasted_iota(jnp.int32, (c, c), 1)
        diff = (i - j).astype(F32)
        k_scale = RET_D ** -0.5
        decay[...] = jnp.where(diff >= 0, jnp.exp(lg * jnp.maximum(diff, 0.0)) * k_scale, 0.0)
        r = lax.broadcasted_iota(jnp.int32, (c, RET_D), 0).astype(F32)
        qdec[...] = jnp.exp(lg * (r + 1.0))
        kdec[...] = jnp.exp(lg * (c - 1.0 - r)) * k_scale

    half = RET_D // 2
    cdec = jnp.exp(jnp.zeros((1, RET_D), F32) + lg * c)
    st = state[...]
    for ci in range(RET_CHUNKS_PER_STEP):
        rs = slice(ci * c, (ci + 1) * c)
        cos = cos_ref[rs, :]
        sin = sin_ref[rs, :]

        def rot(t):
            t1, t2 = t[:, :half], t[:, half:]
            return jnp.concatenate([t1 * cos - t2 * sin, t1 * sin + t2 * cos], axis=1)

        q = rot(q_ref[rs, :].astype(F32))
        k = rot(k_ref[rs, :].astype(F32))
        vb = v_ref[rs, :].astype(BF16)
        qb = q.astype(BF16)
        scores = lax.dot_general(qb, k.astype(BF16), NT_DIMS, preferred_element_type=F32) * decay[...]
        o = (jnp.dot(scores.astype(BF16), vb, preferred_element_type=F32)
             + jnp.dot(qb, st.astype(BF16), preferred_element_type=F32) * qdec[...])
        kd = (k * kdec[...]).astype(BF16)
        st = st * cdec + lax.dot_general(kd, vb, TN_DIMS, preferred_element_type=F32)

        mu = jnp.mean(o, axis=-1, keepdims=True)
        oc = o - mu
        var = jnp.mean(oc * oc, axis=-1, keepdims=True)
        y = oc * lax.rsqrt(var + EPS) * gain_ref[...]
        o_ref[rs, :] = (_silu(g_ref[rs, :].astype(F32)) * y).astype(o_ref.dtype)
    state[...] = st


def retention_heads(proj, cos, sin, gain):
    s = proj.shape[0]
    c = RET_CHUNK
    log_gamma = np.log1p(-np.exp2(-5.0 - np.arange(RET_HEADS, dtype=np.float32))).astype(np.float32)
    col = lambda base: (lambda h, n, lg: (n, base + h))
    rows = min(c * RET_CHUNKS_PER_STEP, s)
    grid_spec = pltpu.PrefetchScalarGridSpec(
        num_scalar_prefetch=1,
        grid=(RET_HEADS, s // rows),
        in_specs=[pl.BlockSpec((rows, RET_D), col(0)),
                  pl.BlockSpec((rows, RET_D), col(RET_HEADS)),
                  pl.BlockSpec((rows, RET_D), col(2 * RET_HEADS)),
                  pl.BlockSpec((rows, RET_D), col(3 * RET_HEADS)),
                  pl.BlockSpec((rows, RET_D // 2), lambda h, n, lg: (n, 0)),
                  pl.BlockSpec((rows, RET_D // 2), lambda h, n, lg: (n, 0)),
                  pl.BlockSpec((1, RET_D), lambda h, n, lg: (0, h))],
        out_specs=pl.BlockSpec((rows, RET_D), lambda h, n, lg: (n, h)),
        scratch_shapes=[pltpu.VMEM((RET_D, RET_D), F32),
                        pltpu.VMEM((c, c), F32),
                        pltpu.VMEM((c, RET_D), F32),
                        pltpu.VMEM((c, RET_D), F32)],
    )
    return pl.pallas_call(
        _ret_kernel,
        grid_spec=grid_spec,
        out_shape=jax.ShapeDtypeStruct((s, RET_WIDTH), BF16),
        compiler_params=_params("arbitrary", "arbitrary"),
        name="retention",
    )(jnp.asarray(log_gamma), proj, proj, proj, proj, cos, sin, gain.reshape(1, RET_WIDTH))


def _split3(x):
    hi = x.astype(BF16)
    r1 = x - hi.astype(F32)
    mid = r1.astype(BF16)
    lo = (r1 - mid.astype(F32)).astype(BF16)
    return hi, mid, lo


def _gla_kernel(q_ref, k_ref, v_ref, g_ref, a_ref, wup_ref, b_ref, gain_ref, o_ref,
                state_t, pair_level, tri, attn_s):
    c = GLA_CHUNK

    @pl.when(pl.program_id(1) == 0)
    def _():
        state_t[...] = jnp.zeros_like(state_t)
        i = lax.broadcasted_iota(jnp.int32, (c, c), 0)
        j = lax.broadcasted_iota(jnp.int32, (c, c), 1)
        tri[...] = jnp.where(i >= j, 1.0, 0.0).astype(BF16)
        x = jnp.where(i > j, jnp.bitwise_xor(i, j), 0)
        lvl = jnp.full((c, c), GLA_LEVELS[-1] - 1, jnp.int32)
        for p in GLA_LEVELS:
            lvl = lvl + jnp.where(x >= (1 << p), 1, 0)
        pair_level[...] = lvl

    w_parts = _split3(wup_ref[...])
    st = state_t[...]
    for ci in range(GLA_CHUNKS_PER_STEP):
        cs = slice(ci * c, (ci + 1) * c)
        st = _gla_chunk(q_ref[cs, :], k_ref[cs, :], v_ref[cs, :], g_ref[cs, :], a_ref[cs, :],
                        w_parts, b_ref[...], gain_ref[...], o_ref.at[cs, :], st,
                        pair_level, tri, attn_s)
    state_t[...] = st


def _gla_chunk(q, k, v, g, a, w_parts, b, gain, o_ref, st, pair_level, tri, attn_s):
    c = GLA_CHUNK
    log2e = 1.4426950408889634
    a_hi, a_mid, _ = _split3(a)
    w_hi, w_mid, _ = w_parts
    dotf = lambda x, y: jnp.dot(x, y, preferred_element_type=F32)
    z = dotf(a_hi, w_hi) + (dotf(a_hi, w_mid) + dotf(a_mid, w_hi)) + b
    la = (jnp.minimum(z, 0.0) - jnp.log(1.0 + jnp.exp(-jnp.abs(z)))) * (log2e / GLA_GATE_TAU)
    l_hi, l_mid, l_lo = _split3(la)
    t = tri[...]
    cum = dotf(t, l_hi) + dotf(t, l_mid) + dotf(t, l_lo)

    qs = q.astype(F32) * (GLA_DK ** -0.5)
    k = k.astype(F32)
    vb = v.astype(BF16)

    o = lax.dot_general((qs * jnp.exp2(cum)).astype(BF16), st.astype(BF16), NT_DIMS,
                        preferred_element_type=F32)

    attn_s[...] = jnp.zeros_like(attn_s)
    for p in GLA_LEVELS:
        half = 1 << p
        nb = c // (2 * half)
        if (2 * half) % LANES == 0:
            for blk in range(nb):
                up = slice(blk * 2 * half, blk * 2 * half + half)
                lo = slice(blk * 2 * half + half, (blk + 1) * 2 * half)
                bnd = cum[up.stop - 1:up.stop, :]
                q_t = (qs[lo] * jnp.exp2(cum[lo] - bnd)).astype(BF16)
                k_t = (k[up] * jnp.exp2(bnd - cum[up])).astype(BF16)
                attn_s[lo, up] = lax.dot_general(q_t, k_t, NT_DIMS, preferred_element_type=F32)
            continue
        q_rows, k_rows = [], []
        for blk in range(nb):
            up = slice(blk * 2 * half, blk * 2 * half + half)
            lo = slice(blk * 2 * half + half, (blk + 1) * 2 * half)
            bnd = cum[up.stop - 1:up.stop, :]
            e_lo = jnp.exp2(cum[lo] - bnd)
            k_rows += [k[up] * jnp.exp2(bnd - cum[up]), k[lo] * e_lo]
            q_rows.append(qs[lo] * e_lo)
        q_t = q_rows[0] if nb == 1 else jnp.concatenate(q_rows, axis=0)
        k_t = jnp.concatenate(k_rows, axis=0)
        part = lax.dot_general(q_t.astype(BF16), k_t.astype(BF16), NT_DIMS,
                               preferred_element_type=F32)
        for blk in range(nb):
            lo = slice(blk * 2 * half + half, (blk + 1) * 2 * half)
            attn_s[lo, :] = jnp.where(pair_level[lo, :] == p,
                                      part[blk * half:(blk + 1) * half, :], attn_s[lo, :])

    rows = lax.broadcasted_iota(jnp.int32, (SUBLANES, 1), 0)
    lane = lax.broadcasted_iota(jnp.int32, (SUBLANES, LANES), 1)
    for blk in range(c // SUBLANES):
        r = blk * SUBLANES
        rs = slice(r, r + SUBLANES)
        ls = slice(r // LANES * LANES, r // LANES * LANES + LANES)
        q8, k8, c8 = qs[rs], k[rs], cum[rs]
        tile = attn_s[rs, ls]
        for m in range(SUBLANES):
            w = jnp.exp2(c8 - c8[m:m + 1, :])
            a = jnp.sum(q8 * w * k8[m:m + 1, :], axis=-1, keepdims=True)
            a = jnp.where(rows >= m, a, 0.0)
            tile = jnp.where(lane == r % LANES + m, a, tile)
        attn_s[rs, ls] = tile
    o = o + jnp.dot(attn_s[...].astype(BF16), vb, preferred_element_type=F32)

    last = cum[c - 1:c, :]
    kd = (k * jnp.exp2(last - cum)).astype(BF16)
    new_st = st * jnp.exp2(last) + lax.dot_general(vb, kd, TN_DIMS, preferred_element_type=F32)

    ms = jnp.mean(o * o, axis=-1, keepdims=True)
    y = o * lax.rsqrt(ms + EPS) * gain
    o_ref[...] = (_silu(g.astype(F32)) * y).astype(o_ref.dtype)
    return new_st


def gla_heads(proj, ga, w_up, b, gain):
    s = proj.shape[0]
    c = GLA_CHUNK
    qk_base = 4 * RET_WIDTH // GLA_DK
    v_base = (4 * RET_WIDTH + 2 * GLA_KEY_WIDTH) // GLA_DV
    w_up_pad = jnp.zeros((GATE_PAD, GLA_KEY_WIDTH), F32).at[:GLA_GATE_RANK].set(w_up)
    rows = c * GLA_CHUNKS_PER_STEP
    return pl.pallas_call(
        _gla_kernel,
        grid=(GLA_HEADS, s // rows),
        in_specs=[pl.BlockSpec((rows, GLA_DK), lambda h, n: (n, qk_base + h)),
                  pl.BlockSpec((rows, GLA_DK), lambda h, n: (n, qk_base + GLA_HEADS + h)),
                  pl.BlockSpec((rows, GLA_DV), lambda h, n: (n, v_base + h)),
                  pl.BlockSpec((rows, GLA_DV), lambda h, n: (n, v_base + GLA_HEADS + h)),
                  pl.BlockSpec((rows, GATE_PAD), lambda h, n: (n, 0)),
                  pl.BlockSpec((GATE_PAD, GLA_DK), lambda h, n: (0, h)),
                  pl.BlockSpec((1, GLA_DK), lambda h, n: (0, h)),
                  pl.BlockSpec((1, GLA_DV), lambda h, n: (0, h))],
        out_specs=pl.BlockSpec((rows, GLA_DV), lambda h, n: (n, h)),
        out_shape=jax.ShapeDtypeStruct((s, GLA_WIDTH), BF16),
        scratch_shapes=[pltpu.VMEM((GLA_DV, GLA_DK), F32),
                        pltpu.VMEM((c, c), jnp.int32),
                        pltpu.VMEM((c, c), BF16),
                        pltpu.VMEM((c, c), F32)],
        compiler_params=_params("arbitrary", "arbitrary"),
        name="gla",
    )(proj, proj, proj, proj, ga, w_up_pad, b.reshape(1, GLA_KEY_WIDTH), gain.reshape(1, GLA_WIDTH))


def kernel(x, positions, mix_norm, w_in, gla_w_up, gla_b, ret_gain, gla_gain, w_out, ffn_norm,
           w_gate, w_up, w_down, final_norm):
    b, s, d = x.shape
    x = x.reshape(b * s, d)
    cos, sin = rope_tables(positions.reshape(b * s))
    w_in_t = jnp.swapaxes(w_in, 1, 2)
    xg, r = norm_inputs(x, mix_norm[0])
    for l in range(DEPTH):
        proj, ga, wd = in_proj(xg, r, w_in_t, w_down, l)
        r_out = retention_heads(proj, cos, sin, ret_gain[l])
        g_out = gla_heads(proj, ga, gla_w_up[l], gla_b[l], gla_gain[l])
        x, xg, r = matmul_resid((r_out, g_out), w_out, x, layer=l, next_gain=ffn_norm[l],
                                tm=2048, single_buffer_a=True, row_splits=4, name="out_proj")
        hid = ffn_up(xg, r, w_gate, w_up, l)
        if l + 1 < DEPTH:
            x, xg, r = matmul_resid((hid,), wd, x, next_gain=mix_norm[l + 1], tm=512, name="ffn_down")
        else:
            x = matmul_resid((hid,), wd, x, tm=512, name="ffn_down")
    return rmsnorm(x, final_norm, F32).reshape(b, s, d)
```

```python
import functools

import numpy as np
import jax
import jax.numpy as jnp
from jax import lax
from jax.experimental import pallas as pl
from jax.experimental.pallas import tpu as pltpu

F32 = jnp.float32
BF16 = jnp.bfloat16

D_MODEL = 4096
DEPTH = 2
RET_HEADS = 8
RET_D = 256
RET_WIDTH = RET_HEADS * RET_D
GLA_HEADS = 4
GLA_DK = 256
GLA_DV = 512
GLA_KEY_WIDTH = GLA_HEADS * GLA_DK
GLA_WIDTH = GLA_HEADS * GLA_DV
GLA_GATE_RANK = 16
GLA_GATE_TAU = 16.0
FFN_HIDDEN = 11008
ROPE_BASE = 10000.0
EPS = 1e-6

MAIN_WIDTH = 4 * RET_WIDTH + 2 * GLA_KEY_WIDTH + 2 * GLA_WIDTH
LANES = 128
SUBLANES = 8
GATE_PAD = LANES

RET_CHUNK = 256
RET_CHUNKS_PER_STEP = 8
GLA_CHUNK = 256
GLA_CHUNKS_PER_STEP = 4
GLA_LEVELS = (7, 6, 5, 4, 3)

VMEM_LIMIT = 62 * 1024 * 1024

NT_DIMS = (((1,), (1,)), ((), ()))
TN_DIMS = (((0,), (0,)), ((), ()))


def _params(*sem):
    return pltpu.CompilerParams(dimension_semantics=sem, vmem_limit_bytes=VMEM_LIMIT)


def _silu(x):
    return x * (1.0 / (1.0 + jnp.exp(-x)))


def _rmsnorm_kernel(x_ref, g_ref, o_ref):
    x = x_ref[...]
    ms = jnp.mean(x * x, axis=-1, keepdims=True)
    o_ref[...] = (x * lax.rsqrt(ms + EPS) * g_ref[...]).astype(o_ref.dtype)


def rmsnorm(x, gain, out_dtype, tm=256):
    s, d = x.shape
    return pl.pallas_call(
        _rmsnorm_kernel,
        grid=(s // tm,),
        in_specs=[pl.BlockSpec((tm, d), lambda i: (i, 0)),
                  pl.BlockSpec((1, d), lambda i: (0, 0))],
        out_specs=pl.BlockSpec((tm, d), lambda i: (i, 0)),
        out_shape=jax.ShapeDtypeStruct((s, d), out_dtype),
        compiler_params=_params("arbitrary"),
        name="rmsnorm",
    )(x, gain.reshape(1, d))


def _fold_lanes(v):
    acc = v[:, :LANES]
    for g in range(1, v.shape[1] // LANES):
        acc = acc + v[:, g * LANES:(g + 1) * LANES]
    return acc


def _row_scale(ss):
    r = lax.rsqrt(jnp.sum(ss, axis=-1, keepdims=True) * (1.0 / D_MODEL) + EPS)
    return jnp.broadcast_to(r, ss.shape)


def _scale_rows(acc, r):
    return acc * jnp.concatenate([r] * (acc.shape[1] // LANES), axis=1)


def _norm_inputs_kernel(x_ref, g_ref, xg_ref, r_ref):
    x = x_ref[...]
    xg_ref[...] = (x * g_ref[...]).astype(BF16)
    r_ref[...] = _row_scale(_fold_lanes(x * x))


def norm_inputs(x, gain, tm=256):
    s, d = x.shape
    return pl.pallas_call(
        _norm_inputs_kernel,
        grid=(s // tm,),
        in_specs=[pl.BlockSpec((tm, d), lambda i: (i, 0)),
                  pl.BlockSpec((1, d), lambda i: (0, 0))],
        out_specs=[pl.BlockSpec((tm, d), lambda i: (i, 0)),
                   pl.BlockSpec((tm, LANES), lambda i: (i, 0))],
        out_shape=[jax.ShapeDtypeStruct((s, d), BF16),
                   jax.ShapeDtypeStruct((s, LANES), F32)],
        compiler_params=_params("arbitrary"),
        name="norm_inputs",
    )(x, gain.reshape(1, d))


def _row_block_spec(tm, k, single_buffer=True):
    return pl.BlockSpec((tm, k), lambda i, j: (i, 0),
                        pipeline_mode=pl.Buffered(1) if single_buffer else None)


def _mm_resid_kernel(*all_refs, n_parts, row_splits):
    operand_refs, res_ref, refs = all_refs[:2 * n_parts], all_refs[2 * n_parts], all_refs[2 * n_parts + 1:]
    emit_norm = len(refs) > 1
    if emit_norm:
        g_ref, o_ref, xg_ref, r_ref = refs
        j = pl.program_id(1)

        @pl.when(j == 0)
        def _():
            r_ref[...] = jnp.zeros_like(r_ref)
    else:
        o_ref, = refs

    ws = [operand_refs[2 * p + 1][...].astype(BF16) for p in range(n_parts)]
    rows = o_ref.shape[0] // row_splits
    for s in range(row_splits):
        rs = slice(s * rows, (s + 1) * rows)
        x = res_ref[rs, :]
        for p in range(n_parts):
            x = x + jnp.dot(operand_refs[2 * p][rs, :], ws[p], preferred_element_type=F32)
        o_ref[rs, :] = x
        if emit_norm:
            xg_ref[rs, :] = (x * g_ref[...]).astype(BF16)
            r_ref[rs, :] += _fold_lanes(x * x)

    if emit_norm:
        @pl.when(j == pl.num_programs(1) - 1)
        def _():
            r_ref[...] = _row_scale(r_ref[...])


def matmul_resid(a_parts, w, resid, layer=None, next_gain=None, tm=1024, tn=512,
                 single_buffer_a=False, row_splits=1, name="matmul"):
    m, k = a_parts[0].shape
    n = w.shape[-1]
    tile = pl.BlockSpec((tm, tn), lambda i, j: (i, j))
    in_specs, args = [], []
    for p, a in enumerate(a_parts):
        in_specs.append(_row_block_spec(tm, k, single_buffer_a))
        if w.ndim == 3:
            in_specs.append(pl.BlockSpec((None, k, tn), lambda i, j, p=p: (layer, p, j)))
        else:
            in_specs.append(pl.BlockSpec((k, tn), lambda i, j, p=p: (p, j)))
        args += [a, w]
    in_specs.append(tile)
    args.append(resid)
    out_specs, out_shape = [tile], [jax.ShapeDtypeStruct((m, n), F32)]
    if next_gain is not None:
        in_specs.append(pl.BlockSpec((1, tn), lambda i, j: (0, j)))
        args.append(next_gain.reshape(1, n))
        out_specs += [tile, pl.BlockSpec((tm, LANES), lambda i, j: (i, 0))]
        out_shape += [jax.ShapeDtypeStruct((m, n), BF16), jax.ShapeDtypeStruct((m, LANES), F32)]
    outs = pl.pallas_call(
        functools.partial(_mm_resid_kernel, n_parts=len(a_parts), row_splits=row_splits),
        grid=(m // tm, n // tn),
        in_specs=in_specs,
        out_specs=out_specs,
        out_shape=out_shape,
        compiler_params=_params("arbitrary", "arbitrary"),
        name=name,
    )(*args)
    return outs if next_gain is not None else outs[0]


def _in_proj_kernel(xg_ref, r_ref, w_ref, wlr_ref, wd_ref, o_ref, ga_ref, wd_bf16_ref):
    acc = lax.dot_general(xg_ref[...], w_ref[...].astype(BF16), NT_DIMS, preferred_element_type=F32)
    o_ref[...] = _scale_rows(acc, r_ref[...]).astype(o_ref.dtype)
    wd_bf16_ref[...] = wd_ref[...].astype(BF16)

    @pl.when(pl.program_id(1) == 0)
    def _():
        wlr = wlr_ref[...]
        row = lax.broadcasted_iota(jnp.int32, wlr.shape, 0)
        wlr = jnp.where(row < GLA_GATE_RANK, wlr, 0.0)
        ga = lax.dot_general(xg_ref[...], wlr.astype(BF16), NT_DIMS, preferred_element_type=F32)
        ga_ref[...] = ga * r_ref[...]


def in_proj(xg, r, w_in_t, w_down, layer, tm=2048, tn=512, tr=128):
    m, k = xg.shape
    _, f, d = w_down.shape
    nj = MAIN_WIDTH // tn
    n_wd = f // tr
    assert n_wd * tr == f and n_wd <= (m // tm) * nj
    wd_block = lambda i, j: jnp.minimum(i * nj + j, n_wd - 1)
    return pl.pallas_call(
        _in_proj_kernel,
        grid=(m // tm, nj),
        in_specs=[_row_block_spec(tm, k),
                  _row_block_spec(tm, LANES),
                  pl.BlockSpec((None, tn, k), lambda i, j: (layer, j, 0)),
                  pl.BlockSpec((None, GATE_PAD, k), lambda i, j: (layer, MAIN_WIDTH // GATE_PAD, 0)),
                  pl.BlockSpec((None, tr, d), lambda i, j: (layer, wd_block(i, j), 0))],
        out_specs=[pl.BlockSpec((tm, tn), lambda i, j: (i, j)),
                   pl.BlockSpec((tm, GATE_PAD), lambda i, j: (i, 0)),
                   pl.BlockSpec((tr, d), lambda i, j: (wd_block(i, j), 0))],
        out_shape=[jax.ShapeDtypeStruct((m, MAIN_WIDTH), BF16),
                   jax.ShapeDtypeStruct((m, GATE_PAD), F32),
                   jax.ShapeDtypeStruct((f, d), BF16)],
        compiler_params=_params("arbitrary", "arbitrary"),
        name="in_proj",
    )(xg, r, w_in_t, w_in_t, w_down)


def _ffn_up_kernel(xg_ref, r_ref, wg_ref, wu_ref, o_ref, *, row_splits):
    wg = wg_ref[...].astype(BF16)
    wu = wu_ref[...].astype(BF16)
    rows = xg_ref.shape[0] // row_splits
    for s in range(row_splits):
        rs = slice(s * rows, (s + 1) * rows)
        g = _scale_rows(jnp.dot(xg_ref[rs, :], wg, preferred_element_type=F32), r_ref[rs, :])
        u = _scale_rows(jnp.dot(xg_ref[rs, :], wu, preferred_element_type=F32), r_ref[rs, :])
        o_ref[rs, :] = (_silu(g) * u).astype(o_ref.dtype)


def ffn_up(xg, r, w_gate, w_up, layer, tm=4096, tn=256, row_splits=4):
    m, k = xg.shape
    n = w_gate.shape[2]
    w_spec = pl.BlockSpec((None, k, tn), lambda i, j: (layer, 0, j))
    return pl.pallas_call(
        functools.partial(_ffn_up_kernel, row_splits=row_splits),
        grid=(m // tm, n // tn),
        in_specs=[_row_block_spec(tm, k), _row_block_spec(tm, LANES), w_spec, w_spec],
        out_specs=pl.BlockSpec((tm, tn), lambda i, j: (i, j)),
        out_shape=jax.ShapeDtypeStruct((m, n), BF16),
        compiler_params=_params("arbitrary", "arbitrary"),
        name="ffn_up",
    )(xg, r, w_gate, w_up)


def _rope_kernel(pos_ref, inv_ref, cos_ref, sin_ref):
    ang = pos_ref[...].astype(F32) * inv_ref[...]
    cos_ref[...] = jnp.cos(ang)
    sin_ref[...] = jnp.sin(ang)


def _prologue_kernel(x_ref, g_ref, pos_ref, inv_ref, xg_ref, r_ref, cos_ref, sin_ref):
    _norm_inputs_kernel(x_ref, g_ref, xg_ref, r_ref)
    _rope_kernel(pos_ref, inv_ref, cos_ref, sin_ref)


def prologue(x, gain, positions, tm=256):
    s, d = x.shape
    half = RET_D // 2
    inv_freq = (ROPE_BASE ** (-np.arange(half, dtype=np.float32) / half)).astype(np.float32)
    rows = lambda width: pl.BlockSpec((tm, width), lambda i: (i, 0))
    return pl.pallas_call(
        _prologue_kernel,
        grid=(s // tm,),
        in_specs=[rows(d), pl.BlockSpec((1, d), lambda i: (0, 0)),
                  rows(1), pl.BlockSpec((1, half), lambda i: (0, 0))],
        out_specs=[rows(d), rows(LANES), rows(half), rows(half)],
        out_shape=[jax.ShapeDtypeStruct((s, d), BF16), jax.ShapeDtypeStruct((s, LANES), F32),
                   jax.ShapeDtypeStruct((s, half), F32), jax.ShapeDtypeStruct((s, half), F32)],
        compiler_params=_params("arbitrary"),
        name="prologue",
    )(x, gain.reshape(1, d), positions.reshape(s, 1), jnp.asarray(inv_freq).reshape(1, half))


def rope_tables(positions, tm=1024):
    s = positions.shape[0]
    tm = min(tm, s)
    half = RET_D // 2
    inv_freq = (ROPE_BASE ** (-np.arange(half, dtype=np.float32) / half)).astype(np.float32)
    return pl.pallas_call(
        _rope_kernel,
        grid=(s // tm,),
        in_specs=[pl.BlockSpec((tm, 1), lambda i: (i, 0)),
                  pl.BlockSpec((1, half), lambda i: (0, 0))],
        out_specs=[pl.BlockSpec((tm, half), lambda i: (i, 0))] * 2,
        out_shape=[jax.ShapeDtypeStruct((s, half), F32)] * 2,
        compiler_params=_params("arbitrary"),
        name="rope_tables",
    )(positions.reshape(s, 1), jnp.asarray(inv_freq).reshape(1, half))


def _ret_kernel(lg_ref, q_ref, k_ref, v_ref, g_ref, cos_ref, sin_ref, gain_ref, o_ref,
                state, decay, qdec, kdec):
    c = RET_CHUNK
    lg = lg_ref[pl.program_id(0)]

    @pl.when(pl.program_id(1) == 0)
    def _():
        state[...] = jnp.zeros_like(state)
        i = lax.broadcasted_iota(jnp.int32, (c, c), 0)
        j = lax.broadcasted_iota(jnp.int32, (c, c), 1)
        diff = (i - j).astype(F32)
        k_scale = RET_D ** -0.5
        decay[...] = jnp.where(diff >= 0, jnp.exp(lg * jnp.maximum(diff, 0.0)) * k_scale, 0.0)
        r = lax.broadcasted_iota(jnp.int32, (c, RET_D), 0).astype(F32)
        qdec[...] = jnp.exp(lg * (r + 1.0))
        kdec[...] = jnp.exp(lg * (c - 1.0 - r)) * k_scale

    half = RET_D // 2
    cdec = jnp.exp(jnp.zeros((1, RET_D), F32) + lg * c)
    st = state[...]
    for ci in range(RET_CHUNKS_PER_STEP):
        rs = slice(ci * c, (ci + 1) * c)
        cos = cos_ref[rs, :]
        sin = sin_ref[rs, :]

        def rot(t):
            t1, t2 = t[:, :half], t[:, half:]
            return jnp.concatenate([t1 * cos - t2 * sin, t1 * sin + t2 * cos], axis=1)

        q = rot(q_ref[rs, :].astype(F32))
        k = rot(k_ref[rs, :].astype(F32))
        vb = v_ref[rs, :].astype(BF16)
        qb = q.astype(BF16)
        scores = lax.dot_general(qb, k.astype(BF16), NT_DIMS, preferred_element_type=F32) * decay[...]
        o = (jnp.dot(scores.astype(BF16), vb, preferred_element_type=F32)
             + jnp.dot(qb, st.astype(BF16), preferred_element_type=F32) * qdec[...])
        kd = (k * kdec[...]).astype(BF16)
        st = st * cdec + lax.dot_general(kd, vb, TN_DIMS, preferred_element_type=F32)

        mu = jnp.mean(o, axis=-1, keepdims=True)
        oc = o - mu
        var = jnp.mean(oc * oc, axis=-1, keepdims=True)
        y = oc * lax.rsqrt(var + EPS) * gain_ref[...]
        o_ref[rs, :] = (_silu(g_ref[rs, :].astype(F32)) * y).astype(o_ref.dtype)
    state[...] = st


def retention_heads(proj, cos, sin, gain):
    s = proj.shape[0]
    c = RET_CHUNK
    log_gamma = np.log1p(-np.exp2(-5.0 - np.arange(RET_HEADS, dtype=np.float32))).astype(np.float32)
    col = lambda base: (lambda h, n, lg: (n, base + h))
    rows = min(c * RET_CHUNKS_PER_STEP, s)
    grid_spec = pltpu.PrefetchScalarGridSpec(
        num_scalar_prefetch=1,
        grid=(RET_HEADS, s // rows),
        in_specs=[pl.BlockSpec((rows, RET_D), col(0)),
                  pl.BlockSpec((rows, RET_D), col(RET_HEADS)),
                  pl.BlockSpec((rows, RET_D), col(2 * RET_HEADS)),
                  pl.BlockSpec((rows, RET_D), col(3 * RET_HEADS)),
                  pl.BlockSpec((rows, RET_D // 2), lambda h, n, lg: (n, 0)),
                  pl.BlockSpec((rows, RET_D // 2), lambda h, n, lg: (n, 0)),
                  pl.BlockSpec((1, RET_D), lambda h, n, lg: (0, h))],
        out_specs=pl.BlockSpec((rows, RET_D), lambda h, n, lg: (n, h)),
        scratch_shapes=[pltpu.VMEM((RET_D, RET_D), F32),
                        pltpu.VMEM((c, c), F32),
                        pltpu.VMEM((c, RET_D), F32),
                        pltpu.VMEM((c, RET_D), F32)],
    )
    return pl.pallas_call(
        _ret_kernel,
        grid_spec=grid_spec,
        out_shape=jax.ShapeDtypeStruct((s, RET_WIDTH), BF16),
        compiler_params=_params("arbitrary", "arbitrary"),
        name="retention",
    )(jnp.asarray(log_gamma), proj, proj, proj, proj, cos, sin, gain.reshape(1, RET_WIDTH))


def _split3(x):
    hi = x.astype(BF16)
    r1 = x - hi.astype(F32)
    mid = r1.astype(BF16)
    lo = (r1 - mid.astype(F32)).astype(BF16)
    return hi, mid, lo


def _gla_kernel(q_ref, k_ref, v_ref, g_ref, a_ref, wup_ref, b_ref, gain_ref, o_ref,
                state_t, pair_level, tri, attn_s):
    c = GLA_CHUNK

    @pl.when(pl.program_id(1) == 0)
    def _():
        state_t[...] = jnp.zeros_like(state_t)
        i = lax.broadcasted_iota(jnp.int32, (c, c), 0)
        j = lax.broadcasted_iota(jnp.int32, (c, c), 1)
        tri[...] = jnp.where(i >= j, 1.0, 0.0).astype(BF16)
        x = jnp.where(i > j, jnp.bitwise_xor(i, j), 0)
        lvl = jnp.full((c, c), GLA_LEVELS[-1] - 1, jnp.int32)
        for p in GLA_LEVELS:
            lvl = lvl + jnp.where(x >= (1 << p), 1, 0)
        pair_level[...] = lvl

    w_parts = _split3(wup_ref[...])
    st = state_t[...]
    for ci in range(GLA_CHUNKS_PER_STEP):
        cs = slice(ci * c, (ci + 1) * c)
        st = _gla_chunk(q_ref[cs, :], k_ref[cs, :], v_ref[cs, :], g_ref[cs, :], a_ref[cs, :],
                        w_parts, b_ref[...], gain_ref[...], o_ref.at[cs, :], st,
                        pair_level, tri, attn_s)
    state_t[...] = st


def _gla_chunk(q, k, v, g, a, w_parts, b, gain, o_ref, st, pair_level, tri, attn_s):
    c = GLA_CHUNK
    log2e = 1.4426950408889634
    a_hi, a_mid, _ = _split3(a)
    w_hi, w_mid, _ = w_parts
    dotf = lambda x, y: jnp.dot(x, y, preferred_element_type=F32)
    z = dotf(a_hi, w_hi) + (dotf(a_hi, w_mid) + dotf(a_mid, w_hi)) + b
    la = (jnp.minimum(z, 0.0) - jnp.log(1.0 + jnp.exp(-jnp.abs(z)))) * (log2e / GLA_GATE_TAU)
    l_hi, l_mid, l_lo = _split3(la)
    t = tri[...]
    cum = dotf(t, l_hi) + dotf(t, l_mid) + dotf(t, l_lo)

    qs = q.astype(F32) * (GLA_DK ** -0.5)
    k = k.astype(F32)
    vb = v.astype(BF16)

    o = lax.dot_general((qs * jnp.exp2(cum)).astype(BF16), st.astype(BF16), NT_DIMS,
                        preferred_element_type=F32)

    attn_s[...] = jnp.zeros_like(attn_s)
    for p in GLA_LEVELS:
        half = 1 << p
        nb = c // (2 * half)
        q_rows, k_rows = [], []
        for blk in range(nb):
            up = slice(blk * 2 * half, blk * 2 * half + half)
            lo = slice(blk * 2 * half + half, (blk + 1) * 2 * half)
            bnd = cum[up.stop - 1:up.stop, :]
            e_lo = jnp.exp2(cum[lo] - bnd)
            k_rows += [k[up] * jnp.exp2(bnd - cum[up]), k[lo] * e_lo]
            q_rows.append(qs[lo] * e_lo)
        q_t = q_rows[0] if nb == 1 else jnp.concatenate(q_rows, axis=0)
        k_t = jnp.concatenate(k_rows, axis=0)
        part = lax.dot_general(q_t.astype(BF16), k_t.astype(BF16), NT_DIMS,
                               preferred_element_type=F32)
        for blk in range(nb):
            lo = slice(blk * 2 * half + half, (blk + 1) * 2 * half)
            attn_s[lo, :] = jnp.where(pair_level[lo, :] == p,
                                      part[blk * half:(blk + 1) * half, :], attn_s[lo, :])

    rows = lax.broadcasted_iota(jnp.int32, (SUBLANES, 1), 0)
    lane = lax.broadcasted_iota(jnp.int32, (SUBLANES, LANES), 1)
    for blk in range(c // SUBLANES):
        r = blk * SUBLANES
        rs = slice(r, r + SUBLANES)
        ls = slice(r // LANES * LANES, r // LANES * LANES + LANES)
        q8, k8, c8 = qs[rs], k[rs], cum[rs]
        tile = attn_s[rs, ls]
        for m in range(SUBLANES):
            w = jnp.exp2(c8 - c8[m:m + 1, :])
            a = jnp.sum(q8 * w * k8[m:m + 1, :], axis=-1, keepdims=True)
            a = jnp.where(rows >= m, a, 0.0)
            tile = jnp.where(lane == r % LANES + m, a, tile)
        attn_s[rs, ls] = tile
    o = o + jnp.dot(attn_s[...].astype(BF16), vb, preferred_element_type=F32)

    last = cum[c - 1:c, :]
    kd = (k * jnp.exp2(last - cum)).astype(BF16)
    new_st = st * jnp.exp2(last) + lax.dot_general(vb, kd, TN_DIMS, preferred_element_type=F32)

    ms = jnp.mean(o * o, axis=-1, keepdims=True)
    y = o * lax.rsqrt(ms + EPS) * gain
    o_ref[...] = (_silu(g.astype(F32)) * y).astype(o_ref.dtype)
    return new_st


def gla_heads(proj, ga, w_up, b, gain):
    s = proj.shape[0]
    c = GLA_CHUNK
    qk_base = 4 * RET_WIDTH // GLA_DK
    v_base = (4 * RET_WIDTH + 2 * GLA_KEY_WIDTH) // GLA_DV
    w_up_pad = jnp.zeros((GATE_PAD, GLA_KEY_WIDTH), F32).at[:GLA_GATE_RANK].set(w_up)
    rows = c * GLA_CHUNKS_PER_STEP
    return pl.pallas_call(
        _gla_kernel,
        grid=(GLA_HEADS, s // rows),
        in_specs=[pl.BlockSpec((rows, GLA_DK), lambda h, n: (n, qk_base + h)),
                  pl.BlockSpec((rows, GLA_DK), lambda h, n: (n, qk_base + GLA_HEADS + h)),
                  pl.BlockSpec((rows, GLA_DV), lambda h, n: (n, v_base + h)),
                  pl.BlockSpec((rows, GLA_DV), lambda h, n: (n, v_base + GLA_HEADS + h)),
                  pl.BlockSpec((rows, GATE_PAD), lambda h, n: (n, 0)),
                  pl.BlockSpec((GATE_PAD, GLA_DK), lambda h, n: (0, h)),
                  pl.BlockSpec((1, GLA_DK), lambda h, n: (0, h)),
                  pl.BlockSpec((1, GLA_DV), lambda h, n: (0, h))],
        out_specs=pl.BlockSpec((rows, GLA_DV), lambda h, n: (n, h)),
        out_shape=jax.ShapeDtypeStruct((s, GLA_WIDTH), BF16),
        scratch_shapes=[pltpu.VMEM((GLA_DV, GLA_DK), F32),
                        pltpu.VMEM((c, c), jnp.int32),
                        pltpu.VMEM((c, c), BF16),
                        pltpu.VMEM((c, c), F32)],
        compiler_params=_params("arbitrary", "arbitrary"),
        name="gla",
    )(proj, proj, proj, proj, ga, w_up_pad, b.reshape(1, GLA_KEY_WIDTH), gain.reshape(1, GLA_WIDTH))


def kernel(x, positions, mix_norm, w_in, gla_w_up, gla_b, ret_gain, gla_gain, w_out, ffn_norm,
           w_gate, w_up, w_down, final_norm):
    b, s, d = x.shape
    x = x.reshape(b * s, d)
    xg, r, cos, sin = prologue(x, mix_norm[0], positions.reshape(b * s))
    w_in_t = jnp.swapaxes(w_in, 1, 2)
    for l in range(DEPTH):
        proj, ga, wd = in_proj(xg, r, w_in_t, w_down, l)
        r_out = retention_heads(proj, cos, sin, ret_gain[l])
        g_out = gla_heads(proj, ga, gla_w_up[l], gla_b[l], gla_gain[l])
        x, xg, r = matmul_resid((r_out, g_out), w_out, x, layer=l, next_gain=ffn_norm[l],
                                tm=2048, single_buffer_a=True, row_splits=4, name="out_proj")
        hid = ffn_up(xg, r, w_gate, w_up, l)
        if l + 1 < DEPTH:
            x, xg, r = matmul_resid((hid,), wd, x, next_gain=mix_norm[l + 1], tm=512, name="ffn_down")
        else:
            x = matmul_resid((hid,), wd, x, tm=512, name="ffn_down")
    return rmsnorm(x, final_norm, F32).reshape(b, s, d)
```
